```python
import math
import jax, jax.numpy as jnp
from jax import lax
import numpy as np

D_MODEL = 2048
BATCH = 4
SEQ = 4096
DEPTH = 1

CONV_WIDTH = D_MODEL // 2
CONV_KERNEL = 31
SSM_WIDTH = D_MODEL // 2
SSM_GROUP = 16
SSM_GROUPS = SSM_WIDTH // SSM_GROUP
SSM_STATE = 64
N_BRANCHES = 2
IN_WIDTH = 2 * CONV_WIDTH + SSM_WIDTH + N_BRANCHES * D_MODEL
PEER_HEADS = 8
PEER_KEYS = 128
PEER_EXPERTS = PEER_KEYS * PEER_KEYS
PEER_QDIM = 256
PEER_HALF = PEER_QDIM // 2
PEER_TOPK = 16
PEER_TOKEN_BLOCK = 128
RMS_EPS = 1e-6
LN_EPS = 1e-5
DT_MIN = 1e-3
DT_MAX = 1e-1

kernel_name = "hybrid_conv_s5_peer_block"


def rmsnorm(x, g):
    xf = x.astype(jnp.float32)
    y = xf * lax.rsqrt(jnp.mean(xf * xf, axis=-1, keepdims=True) + RMS_EPS)
    return (y * g.astype(jnp.float32)).astype(x.dtype)


def layernorm(x, g, b):
    xf = x.astype(jnp.float32)
    mu = jnp.mean(xf, axis=-1, keepdims=True)
    var = jnp.mean(jnp.square(xf - mu), axis=-1, keepdims=True)
    y = (xf - mu) * lax.rsqrt(var + LN_EPS)
    return (y * g.astype(jnp.float32) + b.astype(jnp.float32)).astype(x.dtype)


def conformer_conv(a, w_dw, b_dw, ln_g, ln_b, w_pw):
    glu = a[..., :CONV_WIDTH] * jax.nn.sigmoid(a[..., CONV_WIDTH:])
    padded = jnp.pad(glu, ((0, 0), (CONV_KERNEL - 1, 0), (0, 0)))
    y = lax.conv_general_dilated(
        padded, w_dw.astype(glu.dtype)[:, None, :], window_strides=(1,), padding='VALID',
        dimension_numbers=('NWC', 'WIO', 'NWC'), feature_group_count=CONV_WIDTH)
    y = y + b_dw.astype(glu.dtype)
    y = jax.nn.silu(layernorm(y, ln_g, ln_b))
    return y @ w_pw


def s5_ssm(u, a_re, a_im, log_dt, b_re, b_im, c_re, c_im, d_skip, w_val, w_gate):
    bsz, seq, _ = u.shape
    uf = u.astype(jnp.float32)
    ug = uf.reshape(bsz, seq, SSM_GROUPS, SSM_GROUP)
    lam = lax.complex(a_re.astype(jnp.float32), a_im.astype(jnp.float32))
    dt = jnp.exp(log_dt.astype(jnp.float32))[:, None]
    lam_bar = jnp.exp(lam * dt)
    b_mat = lax.complex(b_re.astype(jnp.float32), b_im.astype(jnp.float32))
    b_bar = ((lam_bar - 1.0) / lam)[..., None] * b_mat
    bu = jnp.einsum('gph,bsgh->bsgp', b_bar, ug.astype(jnp.complex64))
    a_seq = jnp.broadcast_to(lam_bar[None, None], (1, seq, SSM_GROUPS, SSM_STATE))

    def combine(left, right):
        a_l, b_l = left
        a_r, b_r = right
        return a_r * a_l, a_r * b_l + b_r

    _, states = lax.associative_scan(combine, (a_seq, bu), axis=1)
    y = (jnp.einsum('ghp,bsgp->bsgh', c_re.astype(jnp.float32), jnp.real(states))
         - jnp.einsum('ghp,bsgp->bsgh', c_im.astype(jnp.float32), jnp.imag(states)))
    y = y.reshape(bsz, seq, SSM_WIDTH) + d_skip.astype(jnp.float32) * uf
    z = jax.nn.gelu(y).astype(u.dtype)
    return (z @ w_val) * jax.nn.sigmoid(z @ w_gate)


def peer(h, w_q, sub_keys, u_tab, v_tab):
    bsz, seq, d = h.shape
    tokens = h.reshape(bsz * seq // PEER_TOKEN_BLOCK, PEER_TOKEN_BLOCK, d)

    def block(xb):
        q = (xb @ w_q).reshape(PEER_TOKEN_BLOCK, PEER_HEADS, 2, PEER_HALF)
        s = jnp.einsum('thcd,hcnd->thcn', q, sub_keys).astype(jnp.float32)
        s_top, i_top = lax.top_k(s, PEER_TOPK)
        cand = s_top[:, :, 0, :, None] + s_top[:, :, 1, None, :]
        cand_idx = i_top[:, :, 0, :, None] * PEER_KEYS + i_top[:, :, 1, None, :]
        cand = cand.reshape(PEER_TOKEN_BLOCK, PEER_HEADS, PEER_TOPK * PEER_TOPK)
        cand_idx = cand_idx.reshape(PEER_TOKEN_BLOCK, PEER_HEADS, PEER_TOPK * PEER_TOPK)
        best, pos = lax.top_k(cand, PEER_TOPK)
        expert = jnp.take_along_axis(cand_idx, pos, axis=-1)
        gates = jax.nn.softmax(best, axis=-1).astype(xb.dtype)
        u_sel = u_tab[expert]
        v_sel = v_tab[expert]
        act = jax.nn.gelu(jnp.einsum('td,thkd->thk', xb, u_sel))
        return jnp.einsum('thk,thkd->td', gates * act, v_sel)

    out = lax.map(block, tokens)
    return out.reshape(bsz, seq, d)


def setup_inputs(seed: int = 0) -> dict:
    key = jax.random.key(seed)
    ks = jax.random.split(key, 32)
    f32 = jnp.float32
    L, D = DEPTH, D_MODEL

    def nrm(k, shape, scale):
        return jax.random.normal(k, shape, f32) * scale

    n_idx = jnp.arange(SSM_STATE, dtype=f32)
    return {
        "x": nrm(ks[0], (BATCH, SEQ, D), 1.0),
        "norm_mix": 1.0 + nrm(ks[1], (L, D), 0.02),
        "w_in": nrm(ks[2], (L, D, IN_WIDTH), D ** -0.5),
        "b_gate": nrm(ks[3], (L, N_BRANCHES * D), 0.01),
        "conv_w_dw": nrm(ks[4], (L, CONV_KERNEL, CONV_WIDTH), CONV_KERNEL ** -0.5),
        "conv_b_dw": nrm(ks[5], (L, CONV_WIDTH), 0.01),
        "conv_ln_g": 1.0 + nrm(ks[6], (L, CONV_WIDTH), 0.02),
        "conv_ln_b": nrm(ks[7], (L, CONV_WIDTH), 0.01),
        "conv_w_out": nrm(ks[8], (L, CONV_WIDTH, D), CONV_WIDTH ** -0.5),
        "ssm_a_re": -0.5 + nrm(ks[9], (L, SSM_GROUPS, SSM_STATE), 0.01),
        "ssm_a_im": math.pi * n_idx + nrm(ks[10], (L, SSM_GROUPS, SSM_STATE), 0.01),
        "ssm_log_dt": jax.random.uniform(ks[11], (L, SSM_GROUPS), f32, math.log(DT_MIN), math.log(DT_MAX)),
        "ssm_b_re": nrm(ks[12], (L, SSM_GROUPS, SSM_STATE, SSM_GROUP), (2.0 * SSM_GROUP) ** -0.5),
        "ssm_b_im": nrm(ks[13], (L, SSM_GROUPS, SSM_STATE, SSM_GROUP), (2.0 * SSM_GROUP) ** -0.5),
        "ssm_c_re": nrm(ks[14], (L, SSM_GROUPS, SSM_GROUP, SSM_STATE), (2.0 * SSM_STATE) ** -0.5),
        "ssm_c_im": nrm(ks[15], (L, SSM_GROUPS, SSM_GROUP, SSM_STATE), (2.0 * SSM_STATE) ** -0.5),
        "ssm_d": nrm(ks[16], (L, SSM_WIDTH), 1.0),
        "ssm_w_val": nrm(ks[17], (L, SSM_WIDTH, D), SSM_WIDTH ** -0.5),
        "ssm_w_gate": nrm(ks[18], (L, SSM_WIDTH, D), SSM_WIDTH ** -0.5),
        "w_out": nrm(ks[19], (L, D, D), D ** -0.5),
        "norm_ffn": 1.0 + nrm(ks[20], (L, D), 0.02),
        "peer_w_q": nrm(ks[21], (L, D, PEER_HEADS * PEER_QDIM), D ** -0.5),
        "peer_sub_keys": nrm(ks[22], (L, PEER_HEADS, 2, PEER_KEYS, PEER_HALF), PEER_HALF ** -0.5),
        "peer_u": nrm(ks[23], (L, PEER_EXPERTS, D), D ** -0.5),
        "peer_v": nrm(ks[24], (L, PEER_EXPERTS, D), PEER_HEADS ** -0.5),
        "norm_final": 1.0 + nrm(ks[25], (D,), 0.02),
    }


def reference(x, norm_mix, w_in, b_gate, conv_w_dw, conv_b_dw, conv_ln_g, conv_ln_b, conv_w_out,
              ssm_a_re, ssm_a_im, ssm_log_dt, ssm_b_re, ssm_b_im, ssm_c_re, ssm_c_im, ssm_d,
              ssm_w_val, ssm_w_gate, w_out, norm_ffn, peer_w_q, peer_sub_keys, peer_u, peer_v,
              norm_final):
    bsz, seq, _ = x.shape
    for l in range(DEPTH):
        h = rmsnorm(x, norm_mix[l])
        proj = h @ w_in[l]
        conv_in = proj[..., :2 * CONV_WIDTH]
        ssm_in = proj[..., 2 * CONV_WIDTH:2 * CONV_WIDTH + SSM_WIDTH]
        gate_in = proj[..., 2 * CONV_WIDTH + SSM_WIDTH:]
        branch_conv = conformer_conv(conv_in, conv_w_dw[l], conv_b_dw[l], conv_ln_g[l], conv_ln_b[l], conv_w_out[l])
        branch_ssm = s5_ssm(ssm_in, ssm_a_re[l], ssm_a_im[l], ssm_log_dt[l], ssm_b_re[l], ssm_b_im[l],
                            ssm_c_re[l], ssm_c_im[l], ssm_d[l], ssm_w_val[l], ssm_w_gate[l])
        gates = jax.nn.sigmoid(gate_in + b_gate[l]).reshape(bsz, seq, N_BRANCHES, D_MODEL)
        merged = gates[:, :, 0, :] * branch_conv + gates[:, :, 1, :] * branch_ssm
        x = x + merged @ w_out[l]
        h = rmsnorm(x, norm_ffn[l])
        x = x + peer(h, peer_w_q[l], peer_sub_keys[l], peer_u[l], peer_v[l])
    return rmsnorm(x, norm_final)
```

```python
import functools
import math

import jax
import jax.numpy as jnp
from jax import lax
from jax.experimental import pallas as pl
from jax.experimental.pallas import tpu as pltpu

F32 = jnp.float32
BF16 = jnp.bfloat16

RMS_EPS = 1e-6
LN_EPS = 1e-5
PEER_TOPK = 16
LOG2E = 1.0 / math.log(2.0)

SUBLANES = 8
LANES = 128
VMEM_LIMIT_BYTES = 58 * 1024 * 1024


def _params(n_axes):
    return pltpu.CompilerParams(
        dimension_semantics=("arbitrary",) * n_axes,
        vmem_limit_bytes=VMEM_LIMIT_BYTES,
    )


def _rms(x, g):
    ms = jnp.mean(x * x, axis=-1, keepdims=True)
    return x * lax.rsqrt(ms + RMS_EPS) * g


def _in_proj_kernel(x_ref, g_ref, w_ref, o_ref, h_ref):
    @pl.when(pl.program_id(1) == 0)
    def _():
        h_ref[...] = _rms(x_ref[...], g_ref[...]).astype(BF16)

    o_ref[...] = jnp.dot(h_ref[...], w_ref[...], preferred_element_type=F32)


def _in_proj(x2d, g, w, tm, tn):
    t, d = x2d.shape
    n = w.shape[1]
    return pl.pallas_call(
        _in_proj_kernel,
        grid=(t // tm, n // tn),
        in_specs=[
            pl.BlockSpec((tm, d), lambda i, j: (i, 0)),
            pl.BlockSpec((1, d), lambda i, j: (0, 0)),
            pl.BlockSpec((d, tn), lambda i, j: (0, j)),
        ],
        out_specs=pl.BlockSpec((tm, tn), lambda i, j: (i, j)),
        out_shape=jax.ShapeDtypeStruct((t, n), F32),
        scratch_shapes=[pltpu.VMEM((tm, d), BF16)],
        compiler_params=_params(2),
        name="in_proj",
    )(x2d, g.reshape(1, d), w)


CONV_HALO = 32
CONV_ROWS = 64
CONV_COLS = 256


def _conv_kernel(a_ref, b_ref, ah_ref, bh_ref, gate_ref, wdw_ref, bdw_ref, lng_ref, lnb_ref,
                 wpw_ref, bg_ref, o_ref, buf_ref, sh_ref, y_ref):
    ts, c = a_ref.shape
    k_taps = wdw_ref.shape[0]
    n = ts + CONV_HALO
    first = pl.program_id(1) == 0

    glu = a_ref[...] * jax.nn.sigmoid(b_ref[...])
    halo = ah_ref[...] * jax.nn.sigmoid(bh_ref[...])
    buf_ref[0:CONV_HALO, :] = jnp.where(first, 0.0, halo)
    buf_ref[CONV_HALO:n, :] = glu
    buf_ref[n:n + SUBLANES, :] = jnp.zeros((SUBLANES, c), F32)
    for r in range(SUBLANES):
        sh_ref[r] = buf_ref[r:r + n, :]

    off = CONV_HALO - (k_taps - 1)

    def rows(rc, carry):
        r0 = pl.multiple_of(rc * CONV_ROWS, CONV_ROWS)
        for lc in range(c // CONV_COLS):
            cols = slice(lc * CONV_COLS, (lc + 1) * CONV_COLS)
            acc = jnp.zeros((CONV_ROWS, CONV_COLS), F32)
            for k in range(k_taps):
                q, r = divmod(k + off, SUBLANES)
                win = sh_ref[r, pl.ds(r0 + SUBLANES * q, CONV_ROWS), cols]
                acc = acc + win * wdw_ref[k:k + 1, cols]
            y_ref[pl.ds(r0, CONV_ROWS), cols] = acc + bdw_ref[:, cols]
        return carry

    lax.fori_loop(0, ts // CONV_ROWS, rows, 0)

    y = y_ref[...]
    mu = jnp.mean(y, axis=-1, keepdims=True)
    dlt = y - mu
    var = jnp.mean(dlt * dlt, axis=-1, keepdims=True)
    yn = dlt * lax.rsqrt(var + LN_EPS) * lng_ref[...] + lnb_ref[...]
    z = (yn * jax.nn.sigmoid(yn)).astype(BF16)
    out = jnp.dot(z, wpw_ref[...], preferred_element_type=F32)
    o_ref[...] = jax.nn.sigmoid(gate_ref[...] + bg_ref[...]) * out


def _conv_branch(proj, bsz, seq, w_dw, b_dw, ln_g, ln_b, w_pw, b_gate0, ts):
    t = proj.shape[0]
    k_taps, c = w_dw.shape
    d = w_pw.shape[1]
    ns = seq // ts
    hb = ts // CONV_HALO
    assert d == 2 * c and k_taps - 1 <= CONV_HALO and ts % CONV_ROWS == 0 and seq % ts == 0

    def halo_idx(col):
        return lambda b, i: (jnp.maximum((b * ns + i) * hb - 1, 0), col)

    row = lambda a: a.reshape(1, -1)
    return pl.pallas_call(
        _conv_kernel,
        grid=(bsz, ns),
        in_specs=[
            pl.BlockSpec((ts, c), lambda b, i: (b * ns + i, 0)),
            pl.BlockSpec((ts, c), lambda b, i: (b * ns + i, 1)),
            pl.BlockSpec((CONV_HALO, c), halo_idx(0)),
            pl.BlockSpec((CONV_HALO, c), halo_idx(1)),
            pl.BlockSpec((ts, d), lambda b, i: (b * ns + i, 1)),
            pl.BlockSpec((k_taps, c), lambda b, i: (0, 0)),
            pl.BlockSpec((1, c), lambda b, i: (0, 0)),
            pl.BlockSpec((1, c), lambda b, i: (0, 0)),
            pl.BlockSpec((1, c), lambda b, i: (0, 0)),
            pl.BlockSpec((c, d), lambda b, i: (0, 0)),
            pl.BlockSpec((1, d), lambda b, i: (0, 0)),
        ],
        out_specs=pl.BlockSpec((ts, d), lambda b, i: (b * ns + i, 0)),
        out_shape=jax.ShapeDtypeStruct((t, d), F32),
        scratch_shapes=[
            pltpu.VMEM((ts + CONV_HALO + SUBLANES, c), F32),
            pltpu.VMEM((SUBLANES, ts + CONV_HALO, c), F32),
            pltpu.VMEM((ts, c), F32),
        ],
        compiler_params=_params(2),
        name="conv_branch",
    )(proj, proj, proj, proj, proj, w_dw, row(b_dw), row(ln_g), row(ln_b), w_pw, row(b_gate0))


def _ssm_kernel(u_ref, cg_ref, gate_ref, bmat_ref, lam_ref, cmat_ref, dskip_ref, wval_ref,
                wgate_ref, bg_ref, o_ref, lhs_ref, st_ref, slab_ref, carry_ref):
    sl, w = u_ref.shape
    hw = w // 2

    @pl.when(pl.program_id(1) == 0)
    def _():
        carry_ref[...] = jnp.zeros_like(carry_ref)

    sub = lax.broadcasted_iota(jnp.int32, (SUBLANES, w), 0)
    lane_tile = lax.broadcasted_iota(jnp.int32, (SUBLANES, w), 1) // LANES
    keep = sub == lane_tile

    def build(tt, carry):
        t0 = tt * 2
        r0 = jnp.broadcast_to(u_ref[pl.ds(t0, 1), :], (SUBLANES, w))
        r1 = jnp.broadcast_to(u_ref[pl.ds(t0 + 1, 1), :], (SUBLANES, w))
        blk = jnp.concatenate([jnp.where(keep, r0, 0.0), jnp.where(keep, r1, 0.0)], axis=0)
        lhs_ref[pl.ds(pl.multiple_of(tt * 2 * SUBLANES, 2 * SUBLANES), 2 * SUBLANES), :] = blk.astype(BF16)
        return carry

    lax.fori_loop(0, sl // 2, build, 0)
    st_ref[...] = jnp.dot(lhs_ref[...], bmat_ref[...], preferred_element_type=F32)

    lam = lam_ref[...]
    ar = lam[:, :hw]
    ai = lam[:, hw:]

    def step(t, s):
        r = pl.multiple_of(t * SUBLANES, SUBLANES)
        bu = st_ref[pl.ds(r, SUBLANES), :]
        sr = s[:, :hw]
        si = s[:, hw:]
        nr = ar * sr - ai * si + bu[:, :hw]
        ni = ar * si + ai * sr + bu[:, hw:]
        ns = jnp.concatenate([nr, ni], axis=1)
        st_ref[pl.ds(r, SUBLANES), :] = ns
        return ns

    carry_ref[...] = lax.fori_loop(0, sl, step, carry_ref[...], unroll=4)

    out8 = jnp.dot(st_ref[...].astype(BF16), cmat_ref[...], preferred_element_type=F32)
    n_tiles = w // LANES
    for ci in range(n_tiles):
        slab_ref[ci] = out8[:, ci * LANES:(ci + 1) * LANES]
    y = jnp.concatenate(
        [slab_ref[ci, pl.ds(ci, sl, stride=SUBLANES), :] for ci in range(n_tiles)], axis=1)

    u = u_ref[...]
    y = y + dskip_ref[...] * u
    z = jax.nn.gelu(y).astype(BF16)
    val = jnp.dot(z, wval_ref[...], preferred_element_type=F32)
    gt = jnp.dot(z, wgate_ref[...], preferred_element_type=F32)
    branch = val * jax.nn.sigmoid(gt)
    o_ref[...] = cg_ref[...] + jax.nn.sigmoid(gate_ref[...] + bg_ref[...]) * branch


def _ssm_tables(a_re, a_im, log_dt, b_re, b_im, c_re, c_im):
    g, p = a_re.shape
    gs = b_re.shape[-1]
    assert g % SUBLANES == 0
    gl = g // SUBLANES
    lam = lax.complex(a_re.astype(F32), a_im.astype(F32))
    dt = jnp.exp(log_dt.astype(F32))[:, None]
    lam_bar = jnp.exp(lam * dt)
    b_bar = ((lam_bar - 1.0) / lam)[..., None] * lax.complex(b_re.astype(F32), b_im.astype(F32))
    lam_t = jnp.concatenate([jnp.real(lam_bar).reshape(SUBLANES, gl * p),
                             jnp.imag(lam_bar).reshape(SUBLANES, gl * p)], axis=1)
    eye = jnp.eye(gl, dtype=F32)

    def bmap(x):
        x = x.reshape(SUBLANES, gl, p, gs)
        return jnp.einsum('jgph,gk->jghkp', x, eye).reshape(g * gs, gl * p)

    bmat = jnp.concatenate([bmap(jnp.real(b_bar)), bmap(jnp.imag(b_bar))], axis=1)

    def cmap(x):
        x = x.reshape(SUBLANES, gl, gs, p)
        return jnp.einsum('cghp,gk->gpckh', x, eye).reshape(gl * p, g * gs)

    cmat = jnp.concatenate([cmap(c_re.astype(F32)), -cmap(c_im.astype(F32))], axis=0)
    return lam_t, bmat.astype(BF16), cmat.astype(BF16)


def _ssm_branch(proj, conv_gated, bsz, seq, lam_t, bmat, cmat, d_skip, w_val, w_gate, b_gate1, sl):
    t = proj.shape[0]
    w = d_skip.shape[0]
    d = w_val.shape[1]
    ns = seq // sl
    n_tiles = w // LANES
    assert n_tiles == SUBLANES and d == 2 * w and seq % sl == 0 and sl % 2 == 0
    u_col = proj.shape[1] // w - 1
    row = lambda a: a.reshape(1, -1)
    const = lambda b, i: (0, 0)
    return pl.pallas_call(
        _ssm_kernel,
        grid=(bsz, ns),
        in_specs=[
            pl.BlockSpec((sl, w), lambda b, i: (b * ns + i, u_col)),
            pl.BlockSpec((sl, d), lambda b, i: (b * ns + i, 0)),
            pl.BlockSpec((sl, d), lambda b, i: (b * ns + i, 2)),
            pl.BlockSpec((w, w), const),
            pl.BlockSpec((SUBLANES, w), const),
            pl.BlockSpec((w, w), const),
            pl.BlockSpec((1, w), const),
            pl.BlockSpec((w, d), const),
            pl.BlockSpec((w, d), const),
            pl.BlockSpec((1, d), const),
        ],
        out_specs=pl.BlockSpec((sl, d), lambda b, i: (b * ns + i, 0)),
        out_shape=jax.ShapeDtypeStruct((t, d), F32),
        scratch_shapes=[
            pltpu.VMEM((sl * SUBLANES, w), BF16),
            pltpu.VMEM((sl * SUBLANES, w), F32),
            pltpu.VMEM((n_tiles, sl * SUBLANES, LANES), F32),
            pltpu.VMEM((SUBLANES, w), F32),
        ],
        compiler_params=_params(2),
        name="ssm_branch",
    )(proj, conv_gated, proj, bmat, lam_t, cmat, row(d_skip), w_val, w_gate, row(b_gate1))


def _out_proj_kernel(m_ref, x_ref, w_ref, g_ref, x2_ref, h2_ref):
    x2 = x_ref[...] + jnp.dot(m_ref[...].astype(BF16), w_ref[...], preferred_element_type=F32)
    x2_ref[...] = x2
    h2_ref[...] = _rms(x2, g_ref[...]).astype(BF16)


def _out_proj(merged, x2d, w, g, tm):
    t, d = x2d.shape
    return pl.pallas_call(
        _out_proj_kernel,
        grid=(t // tm,),
        in_specs=[
            pl.BlockSpec((tm, d), lambda i: (i, 0)),
            pl.BlockSpec((tm, d), lambda i: (i, 0)),
            pl.BlockSpec((d, d), lambda i: (0, 0)),
            pl.BlockSpec((1, d), lambda i: (0, 0)),
        ],
        out_specs=[pl.BlockSpec((tm, d), lambda i: (i, 0)), pl.BlockSpec((tm, d), lambda i: (i, 0))],
        out_shape=[jax.ShapeDtypeStruct((t, d), F32), jax.ShapeDtypeStruct((t, d), BF16)],
        compiler_params=_params(1),
        name="out_proj",
    )(merged, x2d, w, g.reshape(1, d))


def _top_values(s, k):
    vals = []
    for _ in range(k):
        m = jnp.max(s, axis=0, keepdims=True)
        vals.append(m)
        s = jnp.where(s == m, -jnp.inf, s)
    return vals


def _peer_query_kernel(h2_ref, wq_ref, sk_ref, a1_ref, b2_ref, th_ref, qt_ref, s_ref, v2_ref):
    tt = h2_ref.shape[0]
    n_hc, nk, half = sk_ref.shape
    n_heads = n_hc // 2
    kk = PEER_TOPK + 1
    pad_rows = -(-kk // SUBLANES) * SUBLANES

    qt = lax.dot_general(wq_ref[...], h2_ref[...], (((1,), (1,)), ((), ())),
                         preferred_element_type=F32)
    qt_ref[...] = qt.astype(BF16)
    for hc in range(n_hc):
        s_ref[hc] = jnp.dot(sk_ref[hc], qt_ref[hc * half:(hc + 1) * half, :],
                            preferred_element_type=F32)

    for lt in range(tt // LANES):
        ls = slice(lt * LANES, (lt + 1) * LANES)

        def head(h, carry):
            s1 = s_ref[2 * h, :, ls]
            s2 = s_ref[2 * h + 1, :, ls]
            v1 = _top_values(s1, kk)
            v2 = _top_values(s2, kk)
            v2_ref[...] = jnp.full((pad_rows, LANES), -jnp.inf, F32)
            for r, v in enumerate(v2):
                v2_ref[r:r + 1, :] = v
            blocks = []
            for r1 in range(kk):
                n2 = -(-(kk // (r1 + 1)) // SUBLANES) * SUBLANES
                blocks.append(v1[r1] + v2_ref[0:n2, :])
            cand = jnp.concatenate(blocks, axis=0)
            best = _top_values(cand, kk)
            m = best[0]
            tau = 0.5 * (best[PEER_TOPK - 1] + best[PEER_TOPK])
            z = jnp.sum(jnp.where(cand >= tau, jnp.exp(cand - m), 0.0), axis=0, keepdims=True)
            shift = m * LOG2E + jnp.log2(z)
            a1_ref[h, :, ls] = s1 * LOG2E - shift
            b2_ref[h, :, ls] = s2 * LOG2E
            th_ref[h, :, ls] = jnp.broadcast_to(tau * LOG2E - shift, (SUBLANES, LANES))
            return carry

        lax.fori_loop(0, n_heads, head, 0)


def _peer_query(h2, wq_t, sk, tt):
    t, d = h2.shape
    hq = wq_t.shape[0]
    n_hc, nk, half = sk.shape
    n_heads = n_hc // 2
    pad_rows = -(-(PEER_TOPK + 1) // SUBLANES) * SUBLANES
    return pl.pallas_call(
        _peer_query_kernel,
        grid=(t // tt,),
        in_specs=[
            pl.BlockSpec((tt, d), lambda i: (i, 0)),
            pl.BlockSpec((hq, d), lambda i: (0, 0)),
            pl.BlockSpec((n_hc, nk, half), lambda i: (0, 0, 0)),
        ],
        out_specs=[
            pl.BlockSpec((n_heads, nk, tt), lambda i: (0, 0, i)),
            pl.BlockSpec((n_heads, nk, tt), lambda i: (0, 0, i)),
            pl.BlockSpec((n_heads, SUBLANES, tt), lambda i: (0, 0, i)),
        ],
        out_shape=[
            jax.ShapeDtypeStruct((n_heads, nk, t), F32),
            jax.ShapeDtypeStruct((n_heads, nk, t), F32),
            jax.ShapeDtypeStruct((n_heads, SUBLANES, t), F32),
        ],
        scratch_shapes=[
            pltpu.VMEM((hq, tt), BF16),
            pltpu.VMEM((n_hc, nk, tt), F32),
            pltpu.VMEM((pad_rows, LANES), F32),
        ],
        compiler_params=_params(1),
        name="peer_query",
    )(h2, wq_t, sk)


def _peer_expert_kernel(h2_ref, a1_ref, b2_ref, th_ref, u_ref, vt_ref, x2_ref, gf_ref, o_ref,
                        acc_ref, act_ref, w_ref):
    et = pl.program_id(1)
    e_tile = u_ref.shape[0]
    tt = h2_ref.shape[0]
    n_heads, nk, _ = b2_ref.shape
    n_i1 = e_tile // nk

    @pl.when(et == 0)
    def _():
        acc_ref[...] = jnp.zeros_like(acc_ref)

    act_ref[...] = lax.dot_general(u_ref[...], h2_ref[...], (((1,), (1,)), ((), ())),
                                   preferred_element_type=F32)

    for lt in range(tt // LANES):
        ls = slice(lt * LANES, (lt + 1) * LANES)
        th = [th_ref[h, :, ls] for h in range(n_heads)]
        for i1 in range(n_i1):
            a = [jnp.broadcast_to(a1_ref[h, i1:i1 + 1, ls], (SUBLANES, LANES)) for h in range(n_heads)]
            for blk in range(nk // SUBLANES):
                i2 = slice(blk * SUBLANES, (blk + 1) * SUBLANES)
                g = jnp.zeros((SUBLANES, LANES), F32)
                for h in range(n_heads):
                    dd = a[h] + b2_ref[h, i2, ls]
                    g = g + jnp.where(dd >= th[h], jnp.exp2(dd), 0.0)
                rows = slice(i1 * nk + blk * SUBLANES, i1 * nk + (blk + 1) * SUBLANES)
                w_ref[rows, ls] = g * jax.nn.gelu(act_ref[rows, ls])

    acc_ref[...] += jnp.dot(vt_ref[...], w_ref[...].astype(BF16), preferred_element_type=F32)

    @pl.when(et == pl.num_programs(1) - 1)
    def _():
        x3 = x2_ref[...] + acc_ref[...].T
        o_ref[...] = _rms(x3, gf_ref[...])


def _peer_expert(h2, a1, b2, th, u, vt, x2, g_final, tt, e_tile):
    t, d = h2.shape
    n_heads, nk, _ = a1.shape
    n_exp = u.shape[0]
    n_i1 = e_tile // nk
    assert n_i1 % SUBLANES == 0 or n_i1 == nk
    return pl.pallas_call(
        _peer_expert_kernel,
        grid=(t // tt, n_exp // e_tile),
        in_specs=[
            pl.BlockSpec((tt, d), lambda i, e: (i, 0)),
            pl.BlockSpec((n_heads, n_i1, tt), lambda i, e: (0, e, i)),
            pl.BlockSpec((n_heads, nk, tt), lambda i, e: (0, 0, i)),
            pl.BlockSpec((n_heads, SUBLANES, tt), lambda i, e: (0, 0, i)),
            pl.BlockSpec((e_tile, d), lambda i, e: (e, 0)),
            pl.BlockSpec((d, e_tile), lambda i, e: (0, e)),
            pl.BlockSpec((tt, d), lambda i, e: (i, 0)),
            pl.BlockSpec((1, d), lambda i, e: (0, 0)),
        ],
        out_specs=pl.BlockSpec((tt, d), lambda i, e: (i, 0)),
        out_shape=jax.ShapeDtypeStruct((t, d), F32),
        scratch_shapes=[
            pltpu.VMEM((d, tt), F32),
            pltpu.VMEM((e_tile, tt), F32),
            pltpu.VMEM((e_tile, tt), F32),
        ],
        compiler_params=_params(2),
        name="peer_expert",
    )(h2, a1, b2, th, u, vt, x2, g_final.reshape(1, d))


def _tile(n, pref):
    t = min(n, pref)
    assert n % t == 0
    return t


def kernel(x, norm_mix, w_in, b_gate, conv_w_dw, conv_b_dw, conv_ln_g, conv_ln_b, conv_w_out, ssm_a_re, ssm_a_im, ssm_log_dt, ssm_b_re, ssm_b_im, ssm_c_re, ssm_c_im, ssm_d, ssm_w_val, ssm_w_gate, w_out, norm_ffn, peer_w_q, peer_sub_keys, peer_u, peer_v, norm_final):
    bsz, seq, d = x.shape
    depth = w_in.shape[0]
    t = bsz * seq
    cw = conv_w_dw.shape[-1]
    sw = ssm_d.shape[-1]
    n_heads, _, nk, half = peer_sub_keys.shape[1:]
    assert 2 * cw == d and 2 * sw == d and peer_u.shape[1] == nk * nk

    x2d = x.reshape(t, d).astype(F32)
    for l in range(depth):
        w = w_in[l]
        w_perm = jnp.concatenate([w[:, :2 * cw], w[:, 2 * cw + sw:], w[:, 2 * cw:2 * cw + sw]],
                                 axis=1).astype(BF16)
        proj = _in_proj(x2d, norm_mix[l], w_perm, _tile(t, 1024), 1024)

        conv_gated = _conv_branch(proj, bsz, seq, conv_w_dw[l], conv_b_dw[l], conv_ln_g[l],
                                  conv_ln_b[l], conv_w_out[l].astype(BF16), b_gate[l, :d],
                                  _tile(seq, 256))

        lam_t, bmat, cmat = _ssm_tables(ssm_a_re[l], ssm_a_im[l], ssm_log_dt[l], ssm_b_re[l],
                                        ssm_b_im[l], ssm_c_re[l], ssm_c_im[l])
        merged = _ssm_branch(proj, conv_gated, bsz, seq, lam_t, bmat, cmat, ssm_d[l],
                             ssm_w_val[l].astype(BF16), ssm_w_gate[l].astype(BF16), b_gate[l, d:],
                             _tile(seq, 128))

        x2, h2 = _out_proj(merged, x2d, w_out[l].astype(BF16), norm_ffn[l], _tile(t, 256))

        a1, b2, th = _peer_query(h2, peer_w_q[l].T.astype(BF16),
                                 peer_sub_keys[l].reshape(2 * n_heads, nk, half).astype(BF16),
                                 _tile(t, 256))
        last = l == depth - 1
        g_out = norm_final if last else jnp.ones((d,), F32)
        x2d = _peer_expert(h2, a1, b2, th, peer_u[l].astype(BF16), peer_v[l].T.astype(BF16), x2,
                           g_out, _tile(t, 512), 8 * nk)
        assert last, "stacked layers need the un-normalised residual stream"
    return x2d.reshape(bsz, seq, d).astype(x.dtype)
```

```python
import functools
import math

import jax
import jax.numpy as jnp
from jax import lax
from jax.experimental import pallas as pl
from jax.experimental.pallas import tpu as pltpu

F32 = jnp.float32
BF16 = jnp.bfloat16

RMS_EPS = 1e-6
LN_EPS = 1e-5
PEER_TOPK = 16
LOG2E = 1.0 / math.log(2.0)

SUBLANES = 8
LANES = 128
VMEM_LIMIT_BYTES = 58 * 1024 * 1024


def _params(n_axes, flags=None):
    return pltpu.CompilerParams(
        dimension_semantics=("arbitrary",) * n_axes,
        vmem_limit_bytes=VMEM_LIMIT_BYTES,
        flags=flags,
    )


def _rms(x, g):
    ms = jnp.mean(x * x, axis=-1, keepdims=True)
    return x * lax.rsqrt(ms + RMS_EPS) * g


def _in_proj_kernel(x_ref, g_ref, w_ref, o_ref, h_ref):
    @pl.when(pl.program_id(1) == 0)
    def _():
        h_ref[...] = _rms(x_ref[...], g_ref[...]).astype(BF16)

    o_ref[...] = jnp.dot(h_ref[...], w_ref[...], preferred_element_type=F32)


def _in_proj(x2d, g, w, tm, tn):
    t, d = x2d.shape
    n = w.shape[1]
    return pl.pallas_call(
        _in_proj_kernel,
        grid=(t // tm, n // tn),
        in_specs=[
            pl.BlockSpec((tm, d), lambda i, j: (i, 0)),
            pl.BlockSpec((1, d), lambda i, j: (0, 0)),
            pl.BlockSpec((d, tn), lambda i, j: (0, j)),
        ],
        out_specs=pl.BlockSpec((tm, tn), lambda i, j: (i, j)),
        out_shape=jax.ShapeDtypeStruct((t, n), F32),
        scratch_shapes=[pltpu.VMEM((tm, d), BF16)],
        compiler_params=_params(2),
        name="in_proj",
    )(x2d, g.reshape(1, d), w)


CONV_HALO = 32
CONV_ROWS = 64
CONV_COLS = 256


def _conv_kernel(a_ref, b_ref, ah_ref, bh_ref, gate_ref, wdw_ref, bdw_ref, lng_ref, lnb_ref,
                 wpw_ref, bg_ref, o_ref, buf_ref, sh_ref, y_ref):
    ts, c = a_ref.shape
    k_taps = wdw_ref.shape[0]
    n = ts + CONV_HALO
    first = pl.program_id(1) == 0

    glu = a_ref[...] * jax.nn.sigmoid(b_ref[...])
    halo = ah_ref[...] * jax.nn.sigmoid(bh_ref[...])
    buf_ref[0:CONV_HALO, :] = jnp.where(first, 0.0, halo)
    buf_ref[CONV_HALO:n, :] = glu
    buf_ref[n:n + SUBLANES, :] = jnp.zeros((SUBLANES, c), F32)
    for r in range(SUBLANES):
        sh_ref[r] = buf_ref[r:r + n, :]

    off = CONV_HALO - (k_taps - 1)

    def rows(rc, carry):
        r0 = pl.multiple_of(rc * CONV_ROWS, CONV_ROWS)
        for lc in range(c // CONV_COLS):
            cols = slice(lc * CONV_COLS, (lc + 1) * CONV_COLS)
            acc = jnp.zeros((CONV_ROWS, CONV_COLS), F32)
            for k in range(k_taps):
                q, r = divmod(k + off, SUBLANES)
                win = sh_ref[r, pl.ds(r0 + SUBLANES * q, CONV_ROWS), cols]
                acc = acc + win * wdw_ref[k:k + 1, cols]
            y_ref[pl.ds(r0, CONV_ROWS), cols] = acc + bdw_ref[:, cols]
        return carry

    lax.fori_loop(0, ts // CONV_ROWS, rows, 0)

    y = y_ref[...]
    mu = jnp.mean(y, axis=-1, keepdims=True)
    dlt = y - mu
    var = jnp.mean(dlt * dlt, axis=-1, keepdims=True)
    yn = dlt * lax.rsqrt(var + LN_EPS) * lng_ref[...] + lnb_ref[...]
    z = (yn * jax.nn.sigmoid(yn)).astype(BF16)
    out = jnp.dot(z, wpw_ref[...], preferred_element_type=F32)
    o_ref[...] = jax.nn.sigmoid(gate_ref[...] + bg_ref[...]) * out


def _conv_branch(proj, bsz, seq, w_dw, b_dw, ln_g, ln_b, w_pw, b_gate0, ts):
    t = proj.shape[0]
    k_taps, c = w_dw.shape
    d = w_pw.shape[1]
    ns = seq // ts
    hb = ts // CONV_HALO
    assert d == 2 * c and k_taps - 1 <= CONV_HALO and ts % CONV_ROWS == 0 and seq % ts == 0

    def halo_idx(col):
        return lambda b, i: (jnp.maximum((b * ns + i) * hb - 1, 0), col)

    row = lambda a: a.reshape(1, -1)
    return pl.pallas_call(
        _conv_kernel,
        grid=(bsz, ns),
        in_specs=[
            pl.BlockSpec((ts, c), lambda b, i: (b * ns + i, 0)),
            pl.BlockSpec((ts, c), lambda b, i: (b * ns + i, 1)),
            pl.BlockSpec((CONV_HALO, c), halo_idx(0)),
            pl.BlockSpec((CONV_HALO, c), halo_idx(1)),
            pl.BlockSpec((ts, d), lambda b, i: (b * ns + i, 1)),
            pl.BlockSpec((k_taps, c), lambda b, i: (0, 0)),
            pl.BlockSpec((1, c), lambda b, i: (0, 0)),
            pl.BlockSpec((1, c), lambda b, i: (0, 0)),
            pl.BlockSpec((1, c), lambda b, i: (0, 0)),
            pl.BlockSpec((c, d), lambda b, i: (0, 0)),
            pl.BlockSpec((1, d), lambda b, i: (0, 0)),
        ],
        out_specs=pl.BlockSpec((ts, d), lambda b, i: (b * ns + i, 0)),
        out_shape=jax.ShapeDtypeStruct((t, d), F32),
        scratch_shapes=[
            pltpu.VMEM((ts + CONV_HALO + SUBLANES, c), F32),
            pltpu.VMEM((SUBLANES, ts + CONV_HALO, c), F32),
            pltpu.VMEM((ts, c), F32),
        ],
        compiler_params=_params(2),
        name="conv_branch",
    )(proj, proj, proj, proj, proj, w_dw, row(b_dw), row(ln_g), row(ln_b), w_pw, row(b_gate0))


def _ssm_kernel(u_ref, cg_ref, gate_ref, bmat_ref, lam_ref, cmat_ref, dskip_ref, wval_ref,
                wgate_ref, bg_ref, o_ref, st_ref, carry_ref):
    sl, w = u_ref.shape
    n_tiles = w // LANES
    hw = n_tiles // 2

    @pl.when(pl.program_id(1) == 0)
    def _():
        carry_ref[...] = jnp.zeros_like(carry_ref)

    for j in range(SUBLANES):
        bu = jnp.dot(u_ref[:, j * LANES:(j + 1) * LANES].astype(BF16), bmat_ref[j],
                     preferred_element_type=F32)
        for lt in range(n_tiles):
            st_ref[lt, pl.ds(j, sl, stride=SUBLANES), :] = bu[:, lt * LANES:(lt + 1) * LANES]

    lam = [lam_ref[:, lt * LANES:(lt + 1) * LANES] for lt in range(n_tiles)]

    def step(t, s):
        r = pl.multiple_of(t * SUBLANES, SUBLANES)
        new = [None] * n_tiles
        for q in range(hw):
            sr, si, ar, ai = s[q], s[q + hw], lam[q], lam[q + hw]
            new[q] = ar * sr - ai * si + st_ref[q, pl.ds(r, SUBLANES), :]
            new[q + hw] = ar * si + ai * sr + st_ref[q + hw, pl.ds(r, SUBLANES), :]
        for lt in range(n_tiles):
            st_ref[lt, pl.ds(r, SUBLANES), :] = new[lt]
        return tuple(new)

    s_end = lax.fori_loop(0, sl, step, tuple(carry_ref[lt] for lt in range(n_tiles)), unroll=4)
    for lt in range(n_tiles):
        carry_ref[lt] = s_end[lt]

    ys = []
    for j in range(SUBLANES):
        sj = jnp.concatenate([st_ref[lt, pl.ds(j, sl, stride=SUBLANES), :] for lt in range(n_tiles)],
                             axis=1)
        ys.append(jnp.dot(sj.astype(BF16), cmat_ref[j], preferred_element_type=F32))
    y = jnp.concatenate(ys, axis=1)

    y = y + dskip_ref[...] * u_ref[...]
    z = jax.nn.gelu(y).astype(BF16)
    val = jnp.dot(z, wval_ref[...], preferred_element_type=F32)
    gt = jnp.dot(z, wgate_ref[...], preferred_element_type=F32)
    branch = val * jax.nn.sigmoid(gt)
    o_ref[...] = cg_ref[...] + jax.nn.sigmoid(gate_ref[...] + bg_ref[...]) * branch


def _ssm_discretize_kernel(are_ref, aim_ref, ldt_ref, bre_ref, bim_ref, lre_ref, lim_ref, obr_ref, obi_ref):
    a_re = are_ref[...]
    a_im = aim_ref[...]
    dt = jnp.exp(ldt_ref[...])
    mag = jnp.exp(a_re * dt)
    l_re = mag * jnp.cos(a_im * dt)
    l_im = mag * jnp.sin(a_im * dt)
    den = a_re * a_re + a_im * a_im
    c_re = ((l_re - 1.0) * a_re + l_im * a_im) / den
    c_im = (l_im * a_re - (l_re - 1.0) * a_im) / den
    lre_ref[...] = l_re
    lim_ref[...] = l_im
    obr_ref[...] = c_re * bre_ref[...] - c_im * bim_ref[...]
    obi_ref[...] = c_re * bim_ref[...] + c_im * bre_ref[...]


def _ssm_tables(a_re, a_im, log_dt, b_re, b_im, c_re, c_im):
    g, p = a_re.shape
    gs = b_re.shape[-1]
    assert g % SUBLANES == 0
    gl = g // SUBLANES
    rows = lambda x: jnp.repeat(x.astype(F32), gs, axis=0)
    flat = lambda x: jnp.swapaxes(x.astype(F32), 1, 2).reshape(g * gs, p)
    shape = jax.ShapeDtypeStruct((g * gs, p), F32)
    l_re, l_im, bb_re, bb_im = pl.pallas_call(
        _ssm_discretize_kernel, out_shape=[shape] * 4, name="ssm_discretize",
    )(rows(a_re), rows(a_im), rows(jnp.broadcast_to(log_dt[:, None], (g, p))), flat(b_re), flat(b_im))

    pick = lambda x: x.reshape(g, gs, p)[:, 0, :].reshape(SUBLANES, gl * p)
    lam_t = jnp.concatenate([pick(l_re), pick(l_im)], axis=1)
    eye = jnp.eye(gl, dtype=F32)

    def bmap(x):
        x = x.reshape(SUBLANES, gl, gs, p)
        return jnp.einsum('jghp,gk->jghkp', x, eye).reshape(SUBLANES, gl * gs, gl * p)

    bmat = jnp.concatenate([bmap(bb_re), bmap(bb_im)], axis=2)

    def cmap(x):
        x = x.astype(F32).reshape(SUBLANES, gl, gs, p)
        return jnp.einsum('cghp,gk->cgpkh', x, eye).reshape(SUBLANES, gl * p, gl * gs)

    cmat = jnp.concatenate([cmap(c_re), -cmap(c_im)], axis=1)
    return lam_t, bmat.astype(BF16), cmat.astype(BF16)


def _ssm_branch(proj, conv_gated, bsz, seq, lam_t, bmat, cmat, d_skip, w_val, w_gate, b_gate1, sl):
    t = proj.shape[0]
    w = d_skip.shape[0]
    d = w_val.shape[1]
    ns = seq // sl
    n_tiles = w // LANES
    assert n_tiles == SUBLANES and d == 2 * w and seq % sl == 0
    u_col = proj.shape[1] // w - 1
    row = lambda a: a.reshape(1, -1)
    const = lambda b, i: (0, 0)
    const3 = lambda b, i: (0, 0, 0)
    return pl.pallas_call(
        _ssm_kernel,
        grid=(bsz, ns),
        in_specs=[
            pl.BlockSpec((sl, w), lambda b, i: (b * ns + i, u_col)),
            pl.BlockSpec((sl, d), lambda b, i: (b * ns + i, 0)),
            pl.BlockSpec((sl, d), lambda b, i: (b * ns + i, 2)),
            pl.BlockSpec((SUBLANES, LANES, w), const3),
            pl.BlockSpec((SUBLANES, w), const),
            pl.BlockSpec((SUBLANES, w, LANES), const3),
            pl.BlockSpec((1, w), const),
            pl.BlockSpec((w, d), const),
            pl.BlockSpec((w, d), const),
            pl.BlockSpec((1, d), const),
        ],
        out_specs=pl.BlockSpec((sl, d), lambda b, i: (b * ns + i, 0)),
        out_shape=jax.ShapeDtypeStruct((t, d), F32),
        scratch_shapes=[
            pltpu.VMEM((n_tiles, sl * SUBLANES, LANES), F32),
            pltpu.VMEM((n_tiles, SUBLANES, LANES), F32),
        ],
        compiler_params=_params(2),
        name="ssm_branch",
    )(proj, conv_gated, proj, bmat, lam_t, cmat, row(d_skip), w_val, w_gate, row(b_gate1))


def _out_proj_kernel(m_ref, x_ref, w_ref, g_ref, x2_ref, h2_ref):
    x2 = x_ref[...] + jnp.dot(m_ref[...].astype(BF16), w_ref[...], preferred_element_type=F32)
    x2_ref[...] = x2
    h2_ref[...] = _rms(x2, g_ref[...]).astype(BF16)


def _out_proj(merged, x2d, w, g, tm):
    t, d = x2d.shape
    return pl.pallas_call(
        _out_proj_kernel,
        grid=(t // tm,),
        in_specs=[
            pl.BlockSpec((tm, d), lambda i: (i, 0)),
            pl.BlockSpec((tm, d), lambda i: (i, 0)),
            pl.BlockSpec((d, d), lambda i: (0, 0)),
            pl.BlockSpec((1, d), lambda i: (0, 0)),
        ],
        out_specs=[pl.BlockSpec((tm, d), lambda i: (i, 0)), pl.BlockSpec((tm, d), lambda i: (i, 0))],
        out_shape=[jax.ShapeDtypeStruct((t, d), F32), jax.ShapeDtypeStruct((t, d), BF16)],
        compiler_params=_params(1),
        name="out_proj",
    )(merged, x2d, w, g.reshape(1, d))


def _sort_pairs(n):
    pairs = []

    def merge(lo, cnt, r):
        step = 2 * r
        if step < cnt:
            merge(lo, cnt, step)
            merge(lo + r, cnt, step)
            pairs.extend((i, i + r) for i in range(lo + r, lo + cnt - r, step))
        else:
            pairs.append((lo, lo + r))

    def sort(lo, cnt):
        if cnt > 1:
            sort(lo, cnt // 2)
            sort(lo + cnt // 2, cnt // 2)
            merge(lo, cnt, 1)

    sort(0, n)
    return pairs


def _bitonic_pairs(n):
    pairs = []
    stride = n // 2
    while stride:
        pairs.extend((i, i + stride) for i in range(n) if not i & stride)
        stride //= 2
    return pairs


def _apply_pairs(v, pairs):
    v = list(v)
    for i, j in pairs:
        if v[j] is None:
            continue
        if v[i] is None:
            v[i], v[j] = v[j], None
        else:
            v[i], v[j] = jnp.maximum(v[i], v[j]), jnp.minimum(v[i], v[j])
    return v


def _top_sorted(v, k):
    v = _apply_pairs(v, _sort_pairs(k))
    shift = SUBLANES // 2
    while shift:
        w = [None if x is None else pltpu.roll(x, shift, axis=0) for x in v]
        merged = []
        for i in range(k):
            x, y = v[i], w[k - 1 - i]
            merged.append(y if x is None else x if y is None else jnp.maximum(x, y))
        v = _apply_pairs(merged, _bitonic_pairs(k))
        shift //= 2
    return v


def _stack_sublanes(rows, sub):
    out = rows[0]
    for j in range(1, len(rows)):
        out = jnp.where(sub == j, rows[j], out)
    return out


def _peer_query_kernel(h2_ref, wq_ref, sk_ref, a1_ref, b2_ref, th_ref, qt_ref, s_ref):
    tt = h2_ref.shape[0]
    n_hc, nk, half = sk_ref.shape
    n_heads = n_hc // 2
    k = PEER_TOPK
    assert k == 2 * SUBLANES and nk == k * SUBLANES

    qt = lax.dot_general(wq_ref[...], h2_ref[...], (((1,), (1,)), ((), ())),
                         preferred_element_type=F32)
    qt_ref[...] = qt.astype(BF16)
    for hc in range(n_hc):
        s_ref[hc] = jnp.dot(sk_ref[hc], qt_ref[hc * half:(hc + 1) * half, :],
                            preferred_element_type=F32)

    sub = lax.broadcasted_iota(jnp.int32, (SUBLANES, LANES), 0)

    def sublane_reduce(x, op):
        shift = SUBLANES // 2
        while shift:
            x = op(x, pltpu.roll(x, shift, axis=0))
            shift //= 2
        return x

    def head(h, carry):
        for lt in range(tt // LANES):
            ls = slice(lt * LANES, (lt + 1) * LANES)
            tiles = lambda hc: [s_ref[hc, i * SUBLANES:(i + 1) * SUBLANES, ls] for i in range(k)]
            v1 = _top_sorted(tiles(2 * h), k)
            v2 = _top_sorted(tiles(2 * h + 1), k)
            lo2 = _stack_sublanes(v2[:SUBLANES], sub)
            hi2 = _stack_sublanes(v2[SUBLANES:], sub)
            hi1 = _stack_sublanes(v1[SUBLANES:], sub)
            parts = [(v1[0], lo2), (v1[0], hi2)] + [(v1[r], lo2) for r in range(1, SUBLANES)] \
                + [(hi1, v2[0])]
            cand = [p + q for p, q in parts]
            best = _top_sorted(cand + [None] * (k - len(cand)), k)
            m = best[0]
            tau = best[k - 1]
            picked = [c >= tau for c in cand]
            z = sum(jnp.where(pk, jnp.exp(c - m), 0.0) for pk, c in zip(picked, cand))
            z = sublane_reduce(z, jnp.add)
            shift = m * LOG2E + jnp.log2(z)
            dd = [jnp.where(pk, (p * LOG2E - shift) + q * LOG2E, jnp.inf)
                  for pk, (p, q) in zip(picked, parts)]
            th = dd[0]
            for x in dd[1:]:
                th = jnp.minimum(th, x)
            th_ref[h, :, ls] = sublane_reduce(th, jnp.minimum)
            a1_ref[h, :, ls] = s_ref[2 * h, :, ls] * LOG2E - shift[0:1, :]
            b2_ref[h, :, ls] = s_ref[2 * h + 1, :, ls] * LOG2E
        return carry

    lax.fori_loop(0, n_heads, head, 0)


def _peer_query(h2, wq_t, sk, tt):
    t, d = h2.shape
    hq = wq_t.shape[0]
    n_hc, nk, half = sk.shape
    n_heads = n_hc // 2
    return pl.pallas_call(
        _peer_query_kernel,
        grid=(t // tt,),
        in_specs=[
            pl.BlockSpec((tt, d), lambda i: (i, 0)),
            pl.BlockSpec((hq, d), lambda i: (0, 0)),
            pl.BlockSpec((n_hc, nk, half), lambda i: (0, 0, 0)),
        ],
        out_specs=[
            pl.BlockSpec((n_heads, nk, tt), lambda i: (0, 0, i)),
            pl.BlockSpec((n_heads, nk, tt), lambda i: (0, 0, i)),
            pl.BlockSpec((n_heads, SUBLANES, tt), lambda i: (0, 0, i)),
        ],
        out_shape=[
            jax.ShapeDtypeStruct((n_heads, nk, t), F32),
            jax.ShapeDtypeStruct((n_heads, nk, t), F32),
            jax.ShapeDtypeStruct((n_heads, SUBLANES, t), F32),
        ],
        scratch_shapes=[
            pltpu.VMEM((hq, tt), BF16),
            pltpu.VMEM((n_hc, nk, tt), F32),
        ],
        compiler_params=_params(1),
        name="peer_query",
    )(h2, wq_t, sk)


def _expert_weights(act_ref, w_ref, a1_ref, b2_ref, th_ref, lane_tiles):
    e_tile, tt = act_ref.shape
    n_heads, nk, _ = b2_ref.shape
    pair = 2 * SUBLANES
    for lt in lane_tiles:
        ls = slice(lt * LANES, (lt + 1) * LANES)
        th = [th_ref[h, :, ls] for h in range(n_heads)]
        for i1 in range(e_tile // nk):
            a = [jnp.broadcast_to(a1_ref[h, 0, i1:i1 + 1, ls], (SUBLANES, LANES))
                 for h in range(n_heads)]
            for blk in range(nk // pair):
                halves = []
                for half in range(2):
                    r2 = blk * pair + half * SUBLANES
                    g = None
                    for h in range(n_heads):
                        dd = a[h] + b2_ref[h, r2:r2 + SUBLANES, ls]
                        term = jnp.where(dd >= th[h], jnp.exp2(dd), 0.0)
                        g = term if g is None else g + term
                    r = i1 * nk + r2
                    halves.append(g * jax.nn.gelu(act_ref[r:r + SUBLANES, ls]))
                r = i1 * nk + blk * pair
                w_ref[r:r + pair, ls] = jnp.concatenate(halves, axis=0).astype(BF16)


def _peer_expert_kernel(h2_ref, a1p_ref, b2p_ref, thp_ref, a1c_ref, b2c_ref, thc_ref, u_ref, vt_ref,
                        x2_ref, gf_ref, o_ref, acc_ref, act0_ref, act1_ref, w0_ref, w1_ref, *, n_e):
    g = pl.program_id(0)
    e_tile = act0_ref.shape[0]
    nt_dims = (((1,), (1,)), ((), ()))

    @pl.when(g == 0)
    def _():
        acc_ref[...] = jnp.zeros_like(acc_ref)
        act1_ref[...] = jnp.zeros_like(act1_ref)
        w0_ref[...] = jnp.zeros_like(w0_ref)

    tt, d = h2_ref.shape
    n_lt = tt // LANES

    def tick(u_rows, vt_cols, act_new, act_old, w_old, w_new, a1_ref, b2_ref, th_ref):
        def stage_a(half):
            toks = slice(half * (tt // 2), (half + 1) * (tt // 2))
            act_new[:, toks] = lax.dot_general(u_ref[u_rows, :], h2_ref[toks, :], nt_dims,
                                               preferred_element_type=F32)

        def stage_c(half):
            rows = slice(half * (d // 2), (half + 1) * (d // 2))
            acc_ref[rows, :] += jnp.dot(vt_ref[rows, vt_cols], w_old[...], preferred_element_type=F32)

        pieces = [lambda: stage_c(0), lambda: stage_c(1), lambda: stage_a(0), lambda: stage_a(1)]
        for lt in range(n_lt):
            _expert_weights(act_old, w_new, a1_ref, b2_ref, th_ref, [lt])
            pieces[lt * len(pieces) // n_lt]()

    assert n_lt == 4
    tick(slice(0, e_tile), slice(0, e_tile), act0_ref, act1_ref, w0_ref, w1_ref, a1p_ref, b2p_ref, thp_ref)
    tick(slice(e_tile, 2 * e_tile), slice(e_tile, 2 * e_tile), act1_ref, act0_ref, w1_ref, w0_ref,
         a1c_ref, b2c_ref, thc_ref)

    @pl.when(jnp.logical_and(g > 0, (2 * g) % n_e == 0))
    def _():
        x3 = x2_ref[...] + acc_ref[...].T
        o_ref[...] = _rms(x3, gf_ref[...])
        acc_ref[...] = jnp.zeros_like(acc_ref)


def _peer_expert(h2, a1, b2, th, u, vt, x2, g_final, tt, e_tile):
    t, d = h2.shape
    n_heads, nk, _ = a1.shape
    n_exp = u.shape[0]
    n_t = t // tt
    n_e = n_exp // e_tile
    assert n_e % 2 == 0 and e_tile % nk == 0
    half = n_e // 2
    steps = n_t * half + 1
    once = pl.Buffered(1)

    tok_a = lambda g: jnp.minimum((2 * g) // n_e, n_t - 1)
    tok_p = lambda g: jnp.maximum(2 * g - 1, 0) // n_e
    tok_c = lambda g: jnp.maximum(2 * g - 2, 0) // n_e
    meta = lambda shape, tok: pl.BlockSpec(shape, lambda g: (0, 0, tok(g)), pipeline_mode=once)
    n_i1 = e_tile // nk
    a1_tiles = a1.reshape(n_heads, nk // n_i1, n_i1, t)
    a1_spec = lambda tick, tok: pl.BlockSpec((n_heads, 1, n_i1, tt),
                                             lambda g: (0, tick(g) % n_e, 0, tok(g)))
    return pl.pallas_call(
        functools.partial(_peer_expert_kernel, n_e=n_e),
        grid=(steps,),
        in_specs=[
            pl.BlockSpec((tt, d), lambda g: (tok_a(g), 0), pipeline_mode=once),
            a1_spec(lambda g: 2 * g - 1, tok_p), meta((n_heads, nk, tt), tok_p), meta((n_heads, SUBLANES, tt), tok_p),
            a1_spec(lambda g: 2 * g, tok_a), meta((n_heads, nk, tt), tok_a), meta((n_heads, SUBLANES, tt), tok_a),
            pl.BlockSpec((2 * e_tile, d), lambda g: (g % half, 0)),
            pl.BlockSpec((d, 2 * e_tile), lambda g: (0, jnp.maximum(g - 1, 0) % half)),
            pl.BlockSpec((tt, d), lambda g: (tok_c(g), 0), pipeline_mode=once),
            pl.BlockSpec((1, d), lambda g: (0, 0)),
        ],
        out_specs=pl.BlockSpec((tt, d), lambda g: (tok_c(g), 0)),
        out_shape=jax.ShapeDtypeStruct((t, d), F32),
        scratch_shapes=[
            pltpu.VMEM((d, tt), F32),
            pltpu.VMEM((e_tile, tt), F32),
            pltpu.VMEM((e_tile, tt), F32),
            pltpu.VMEM((e_tile, tt), BF16),
            pltpu.VMEM((e_tile, tt), BF16),
        ],
        compiler_params=_params(1),
        name="peer_expert",
    )(h2, a1_tiles, b2, th, a1_tiles, b2, th, u, vt, x2, g_final.reshape(1, d))


def _tile(n, pref):
    t = min(n, pref)
    assert n % t == 0
    return t


def kernel(x, norm_mix, w_in, b_gate, conv_w_dw, conv_b_dw, conv_ln_g, conv_ln_b, conv_w_out, ssm_a_re, ssm_a_im, ssm_log_dt, ssm_b_re, ssm_b_im, ssm_c_re, ssm_c_im, ssm_d, ssm_w_val, ssm_w_gate, w_out, norm_ffn, peer_w_q, peer_sub_keys, peer_u, peer_v, norm_final):
    bsz, seq, d = x.shape
    depth = w_in.shape[0]
    t = bsz * seq
    cw = conv_w_dw.shape[-1]
    sw = ssm_d.shape[-1]
    n_heads, _, nk, half = peer_sub_keys.shape[1:]
    assert 2 * cw == d and 2 * sw == d and peer_u.shape[1] == nk * nk

    x2d = x.reshape(t, d).astype(F32)
    for l in range(depth):
        w = w_in[l]
        w_perm = jnp.concatenate([w[:, :2 * cw], w[:, 2 * cw + sw:], w[:, 2 * cw:2 * cw + sw]],
                                 axis=1).astype(BF16)
        proj = _in_proj(x2d, norm_mix[l], w_perm, _tile(t, 1024), 1024)

        conv_gated = _conv_branch(proj, bsz, seq, conv_w_dw[l], conv_b_dw[l], conv_ln_g[l],
                                  conv_ln_b[l], conv_w_out[l].astype(BF16), b_gate[l, :d],
                                  _tile(seq, 256))

        lam_t, bmat, cmat = _ssm_tables(ssm_a_re[l], ssm_a_im[l], ssm_log_dt[l], ssm_b_re[l],
                                        ssm_b_im[l], ssm_c_re[l], ssm_c_im[l])
        merged = _ssm_branch(proj, conv_gated, bsz, seq, lam_t, bmat, cmat, ssm_d[l],
                             ssm_w_val[l].astype(BF16), ssm_w_gate[l].astype(BF16), b_gate[l, d:],
                             _tile(seq, 256))

        x2, h2 = _out_proj(merged, x2d, w_out[l].astype(BF16), norm_ffn[l], _tile(t, 256))

        a1, b2, th = _peer_query(h2, peer_w_q[l].T.astype(BF16),
                                 peer_sub_keys[l].reshape(2 * n_heads, nk, half).astype(BF16),
                                 _tile(t, 256))
        last = l == depth - 1
        g_out = norm_final if last else jnp.ones((d,), F32)
        x2d = _peer_expert(h2, a1, b2, th, peer_u[l].astype(BF16), peer_v[l].T.astype(BF16), x2,
                           g_out, _tile(t, 512), 4 * nk)
        assert last, "stacked layers need the un-normalised residual stream"
    return x2d.reshape(bsz, seq, d).astype(x.dtype)
```

```python
import math

import jax
import jax.numpy as jnp
from jax import lax
from jax.experimental import pallas as pl
from jax.experimental.pallas import tpu as pltpu

F32 = jnp.float32
BF16 = jnp.bfloat16

RMS_EPS = 1e-6
LN_EPS = 1e-5
PEER_TOPK = 16
LOG2E = 1.0 / math.log(2.0)

SUBLANES = 8
LANES = 128
VMEM_LIMIT_BYTES = 58 * 1024 * 1024


def _params(n_axes):
    return pltpu.CompilerParams(
        dimension_semantics=("arbitrary",) * n_axes,
        vmem_limit_bytes=VMEM_LIMIT_BYTES,
    )


def _rms(x, g):
    ms = jnp.mean(x * x, axis=-1, keepdims=True)
    return x * lax.rsqrt(ms + RMS_EPS) * g


def _in_proj_kernel(x_ref, g_ref, w_ref, o_ref, h_ref):
    @pl.when(pl.program_id(1) == 0)
    def _():
        h_ref[...] = _rms(x_ref[...], g_ref[...]).astype(BF16)

    o_ref[...] = jnp.dot(h_ref[...], w_ref[...], preferred_element_type=F32)


def _in_proj(x2d, g, w, tm, tn):
    t, d = x2d.shape
    n = w.shape[1]
    return pl.pallas_call(
        _in_proj_kernel,
        grid=(t // tm, n // tn),
        in_specs=[
            pl.BlockSpec((tm, d), lambda i, j: (i, 0)),
            pl.BlockSpec((1, d), lambda i, j: (0, 0)),
            pl.BlockSpec((d, tn), lambda i, j: (0, j)),
        ],
        out_specs=pl.BlockSpec((tm, tn), lambda i, j: (i, j)),
        out_shape=jax.ShapeDtypeStruct((t, n), F32),
        scratch_shapes=[pltpu.VMEM((tm, d), BF16)],
        compiler_params=_params(2),
        name="in_proj",
    )(x2d, g.reshape(1, d), w)


CONV_HALO = 32
CONV_ROWS = 32
CONV_COLS = 512


def _conv_kernel(a_ref, b_ref, ah_ref, bh_ref, gate_ref, wdw_ref, bdw_ref, lng_ref, lnb_ref,
                 wpw_ref, bg_ref, o_ref, buf_ref, sh_ref, y_ref):
    ts, c = a_ref.shape
    k_taps = wdw_ref.shape[0]
    n = ts + CONV_HALO
    first = pl.program_id(1) == 0

    glu = a_ref[...] * jax.nn.sigmoid(b_ref[...])
    halo = ah_ref[...] * jax.nn.sigmoid(bh_ref[...])
    buf_ref[0:CONV_HALO, :] = jnp.where(first, 0.0, halo)
    buf_ref[CONV_HALO:n, :] = glu
    buf_ref[n:n + SUBLANES, :] = jnp.zeros((SUBLANES, c), F32)
    for r in range(1, SUBLANES):
        sh_ref[r - 1] = buf_ref[r:r + n, :]

    off = CONV_HALO - (k_taps - 1)

    def rows(rc, carry):
        r0 = pl.multiple_of(rc * CONV_ROWS, CONV_ROWS)
        for lc in range(c // CONV_COLS):
            cols = slice(lc * CONV_COLS, (lc + 1) * CONV_COLS)
            acc = [None] * (CONV_ROWS // SUBLANES)
            for k in range(k_taps):
                q, r = divmod(k + off, SUBLANES)
                wk = wdw_ref[k, :, cols]
                for rt in range(len(acc)):
                    rows_k = pl.ds(r0 + SUBLANES * (q + rt), SUBLANES)
                    win = buf_ref[rows_k, cols] if r == 0 else sh_ref[r - 1, rows_k, cols]
                    acc[rt] = win * wk if acc[rt] is None else acc[rt] + win * wk
            for rt in range(len(acc)):
                y_ref[pl.ds(r0 + SUBLANES * rt, SUBLANES), cols] = acc[rt] + bdw_ref[:, cols]
        return carry

    lax.fori_loop(0, ts // CONV_ROWS, rows, 0)

    y = y_ref[...]
    mu = jnp.mean(y, axis=-1, keepdims=True)
    dlt = y - mu
    var = jnp.mean(dlt * dlt, axis=-1, keepdims=True)
    yn = dlt * lax.rsqrt(var + LN_EPS) * lng_ref[...] + lnb_ref[...]
    z = (yn * jax.nn.sigmoid(yn)).astype(BF16)
    out = jnp.dot(z, wpw_ref[...], preferred_element_type=F32)
    o_ref[...] = jax.nn.sigmoid(gate_ref[...] + bg_ref[...]) * out


def _conv_branch(proj, bsz, seq, w_dw, b_dw, ln_g, ln_b, w_pw, b_gate0, ts):
    t = proj.shape[0]
    k_taps, c = w_dw.shape
    d = w_pw.shape[1]
    ns = seq // ts
    hb = ts // CONV_HALO
    assert d == 2 * c and k_taps - 1 <= CONV_HALO and ts % CONV_ROWS == 0 and seq % ts == 0

    def halo_idx(col):
        return lambda b, i: (jnp.maximum((b * ns + i) * hb - 1, 0), col)

    row = lambda a: a.reshape(1, -1)
    return pl.pallas_call(
        _conv_kernel,
        grid=(bsz, ns),
        in_specs=[
            pl.BlockSpec((ts, c), lambda b, i: (b * ns + i, 0)),
            pl.BlockSpec((ts, c), lambda b, i: (b * ns + i, 1)),
            pl.BlockSpec((CONV_HALO, c), halo_idx(0)),
            pl.BlockSpec((CONV_HALO, c), halo_idx(1)),
            pl.BlockSpec((ts, d), lambda b, i: (b * ns + i, 1)),
            pl.BlockSpec((k_taps, SUBLANES, c), lambda b, i: (0, 0, 0)),
            pl.BlockSpec((1, c), lambda b, i: (0, 0)),
            pl.BlockSpec((1, c), lambda b, i: (0, 0)),
            pl.BlockSpec((1, c), lambda b, i: (0, 0)),
            pl.BlockSpec((c, d), lambda b, i: (0, 0)),
            pl.BlockSpec((1, d), lambda b, i: (0, 0)),
        ],
        out_specs=pl.BlockSpec((ts, d), lambda b, i: (b * ns + i, 0)),
        out_shape=jax.ShapeDtypeStruct((t, d), F32),
        scratch_shapes=[
            pltpu.VMEM((ts + CONV_HALO + SUBLANES, c), F32),
            pltpu.VMEM((SUBLANES - 1, ts + CONV_HALO, c), F32),
            pltpu.VMEM((ts, c), F32),
        ],
        compiler_params=_params(2),
        name="conv_branch",
    )(proj, proj, proj, proj, proj, jnp.broadcast_to(w_dw[:, None, :], (k_taps, SUBLANES, c)),
      row(b_dw), row(ln_g), row(ln_b), w_pw, row(b_gate0))


def _ssm_kernel(u_ref, cg_ref, gate_ref, bmat_ref, lam_ref, cmat_ref, dskip_ref, wval_ref,
                wgate_ref, bg_ref, o_ref, st_ref, carry_ref):
    sl, w = u_ref.shape
    n_tiles = w // LANES
    hw = n_tiles // 2

    @pl.when(pl.program_id(1) == 0)
    def _():
        carry_ref[...] = jnp.zeros_like(carry_ref)

    for j in range(SUBLANES):
        bu = jnp.dot(u_ref[:, j * LANES:(j + 1) * LANES].astype(BF16), bmat_ref[j],
                     preferred_element_type=F32)
        for lt in range(n_tiles):
            st_ref[lt, pl.ds(j, sl, stride=SUBLANES), :] = bu[:, lt * LANES:(lt + 1) * LANES]

    lam = [lam_ref[:, lt * LANES:(lt + 1) * LANES] for lt in range(n_tiles)]

    def step(t, s):
        r = pl.multiple_of(t * SUBLANES, SUBLANES)
        new = [None] * n_tiles
        for q in range(hw):
            sr, si, ar, ai = s[q], s[q + hw], lam[q], lam[q + hw]
            new[q] = ar * sr - ai * si + st_ref[q, pl.ds(r, SUBLANES), :]
            new[q + hw] = ar * si + ai * sr + st_ref[q + hw, pl.ds(r, SUBLANES), :]
        for lt in range(n_tiles):
            st_ref[lt, pl.ds(r, SUBLANES), :] = new[lt]
        return tuple(new)

    s_end = lax.fori_loop(0, sl, step, tuple(carry_ref[lt] for lt in range(n_tiles)), unroll=4)
    for lt in range(n_tiles):
        carry_ref[lt] = s_end[lt]

    ys = []
    for j in range(SUBLANES):
        sj = jnp.concatenate([st_ref[lt, pl.ds(j, sl, stride=SUBLANES), :] for lt in range(n_tiles)],
                             axis=1)
        ys.append(jnp.dot(sj.astype(BF16), cmat_ref[j], preferred_element_type=F32))
    y = jnp.concatenate(ys, axis=1)

    y = y + dskip_ref[...] * u_ref[...]
    z = jax.nn.gelu(y).astype(BF16)
    val = jnp.dot(z, wval_ref[...], preferred_element_type=F32)
    gt = jnp.dot(z, wgate_ref[...], preferred_element_type=F32)
    branch = val * jax.nn.sigmoid(gt)
    o_ref[...] = cg_ref[...] + jax.nn.sigmoid(gate_ref[...] + bg_ref[...]) * branch


def _ssm_discretize_kernel(are_ref, aim_ref, ldt_ref, bre_ref, bim_ref, lre_ref, lim_ref, obr_ref, obi_ref):
    a_re = are_ref[...]
    a_im = aim_ref[...]
    dt = jnp.exp(ldt_ref[...])
    mag = jnp.exp(a_re * dt)
    l_re = mag * jnp.cos(a_im * dt)
    l_im = mag * jnp.sin(a_im * dt)
    den = a_re * a_re + a_im * a_im
    c_re = ((l_re - 1.0) * a_re + l_im * a_im) / den
    c_im = (l_im * a_re - (l_re - 1.0) * a_im) / den
    lre_ref[...] = l_re
    lim_ref[...] = l_im
    obr_ref[...] = c_re * bre_ref[...] - c_im * bim_ref[...]
    obi_ref[...] = c_re * bim_ref[...] + c_im * bre_ref[...]


def _ssm_tables(a_re, a_im, log_dt, b_re, b_im, c_re, c_im):
    g, p = a_re.shape
    gs = b_re.shape[-1]
    assert g % SUBLANES == 0
    gl = g // SUBLANES
    rows = lambda x: jnp.repeat(x.astype(F32), gs, axis=0)
    flat = lambda x: jnp.swapaxes(x.astype(F32), 1, 2).reshape(g * gs, p)
    shape = jax.ShapeDtypeStruct((g * gs, p), F32)
    l_re, l_im, bb_re, bb_im = pl.pallas_call(
        _ssm_discretize_kernel, out_shape=[shape] * 4, name="ssm_discretize",
    )(rows(a_re), rows(a_im), rows(jnp.broadcast_to(log_dt[:, None], (g, p))), flat(b_re), flat(b_im))

    pick = lambda x: x.reshape(g, gs, p)[:, 0, :].reshape(SUBLANES, gl * p)
    lam_t = jnp.concatenate([pick(l_re), pick(l_im)], axis=1)
    eye = jnp.eye(gl, dtype=F32)

    def bmap(x):
        x = x.reshape(SUBLANES, gl, gs, p)
        return jnp.einsum('jghp,gk->jghkp', x, eye).reshape(SUBLANES, gl * gs, gl * p)

    bmat = jnp.concatenate([bmap(bb_re), bmap(bb_im)], axis=2)

    def cmap(x):
        x = x.astype(F32).reshape(SUBLANES, gl, gs, p)
        return jnp.einsum('cghp,gk->cgpkh', x, eye).reshape(SUBLANES, gl * p, gl * gs)

    cmat = jnp.concatenate([cmap(c_re), -cmap(c_im)], axis=1)
    return lam_t, bmat.astype(BF16), cmat.astype(BF16)


def _ssm_branch(proj, conv_gated, bsz, seq, lam_t, bmat, cmat, d_skip, w_val, w_gate, b_gate1, sl):
    t = proj.shape[0]
    w = d_skip.shape[0]
    d = w_val.shape[1]
    ns = seq // sl
    n_tiles = w // LANES
    assert n_tiles == SUBLANES and d == 2 * w and seq % sl == 0
    u_col = proj.shape[1] // w - 1
    row = lambda a: a.reshape(1, -1)
    const = lambda b, i: (0, 0)
    const3 = lambda b, i: (0, 0, 0)
    return pl.pallas_call(
        _ssm_kernel,
        grid=(bsz, ns),
        in_specs=[
            pl.BlockSpec((sl, w), lambda b, i: (b * ns + i, u_col)),
            pl.BlockSpec((sl, d), lambda b, i: (b * ns + i, 0)),
            pl.BlockSpec((sl, d), lambda b, i: (b * ns + i, 2)),
            pl.BlockSpec((SUBLANES, LANES, w), const3),
            pl.BlockSpec((SUBLANES, w), const),
            pl.BlockSpec((SUBLANES, w, LANES), const3),
            pl.BlockSpec((1, w), const),
            pl.BlockSpec((w, d), const),
            pl.BlockSpec((w, d), const),
            pl.BlockSpec((1, d), const),
        ],
        out_specs=pl.BlockSpec((sl, d), lambda b, i: (b * ns + i, 0)),
        out_shape=jax.ShapeDtypeStruct((t, d), F32),
        scratch_shapes=[
            pltpu.VMEM((n_tiles, sl * SUBLANES, LANES), F32),
            pltpu.VMEM((n_tiles, SUBLANES, LANES), F32),
        ],
        compiler_params=_params(2),
        name="ssm_branch",
    )(proj, conv_gated, proj, bmat, lam_t, cmat, row(d_skip), w_val, w_gate, row(b_gate1))


def _out_proj_kernel(m_ref, x_ref, w_ref, g_ref, x2_ref, h2t_ref):
    x2 = x_ref[...] + jnp.dot(m_ref[...].astype(BF16), w_ref[...], preferred_element_type=F32)
    x2_ref[...] = x2
    h2t_ref[...] = _rms(x2, g_ref[...]).T.astype(BF16)


def _out_proj(merged, x2d, w, g, tm):
    t, d = x2d.shape
    return pl.pallas_call(
        _out_proj_kernel,
        grid=(t // tm,),
        in_specs=[
            pl.BlockSpec((tm, d), lambda i: (i, 0)),
            pl.BlockSpec((tm, d), lambda i: (i, 0)),
            pl.BlockSpec((d, d), lambda i: (0, 0)),
            pl.BlockSpec((1, d), lambda i: (0, 0)),
        ],
        out_specs=[pl.BlockSpec((tm, d), lambda i: (i, 0)), pl.BlockSpec((d, tm), lambda i: (0, i))],
        out_shape=[jax.ShapeDtypeStruct((t, d), F32), jax.ShapeDtypeStruct((d, t), BF16)],
        compiler_params=_params(1),
        name="out_proj",
    )(merged, x2d, w, g.reshape(1, d))


def _sort_pairs(n):
    pairs = []

    def merge(lo, cnt, r):
        step = 2 * r
        if step < cnt:
            merge(lo, cnt, step)
            merge(lo + r, cnt, step)
            pairs.extend((i, i + r) for i in range(lo + r, lo + cnt - r, step))
        else:
            pairs.append((lo, lo + r))

    def sort(lo, cnt):
        if cnt > 1:
            sort(lo, cnt // 2)
            sort(lo + cnt // 2, cnt // 2)
            merge(lo, cnt, 1)

    sort(0, n)
    return pairs


def _bitonic_pairs(n):
    pairs = []
    stride = n // 2
    while stride:
        pairs.extend((i, i + stride) for i in range(n) if not i & stride)
        stride //= 2
    return pairs


def _apply_pairs(v, pairs):
    v = list(v)
    for i, j in pairs:
        if v[j] is None:
            continue
        if v[i] is None:
            v[i], v[j] = v[j], None
        else:
            v[i], v[j] = jnp.maximum(v[i], v[j]), jnp.minimum(v[i], v[j])
    return v


def _top_sorted(v, k):
    v = _apply_pairs(v, _sort_pairs(k))
    shift = SUBLANES // 2
    while shift:
        w = [None if x is None else pltpu.roll(x, shift, axis=0) for x in v]
        merged = []
        for i in range(k):
            x, y = v[i], w[k - 1 - i]
            merged.append(y if x is None else x if y is None else jnp.maximum(x, y))
        v = _apply_pairs(merged, _bitonic_pairs(k))
        shift //= 2
    return v


def _stack_sublanes(rows, sub):
    out = rows[0]
    for j in range(1, len(rows)):
        out = jnp.where(sub == j, rows[j], out)
    return out


def _peer_query_kernel(h2t_ref, wq_ref, sk_ref, a1_ref, b2_ref, th_ref, qt_ref, s_ref):
    tt = h2t_ref.shape[1]
    n_hc, nk, half = sk_ref.shape
    n_heads = n_hc // 2
    k = PEER_TOPK
    assert k == 2 * SUBLANES and nk == k * SUBLANES

    qt_ref[...] = jnp.dot(wq_ref[...], h2t_ref[...], preferred_element_type=F32).astype(BF16)
    for hc in range(n_hc):
        s_ref[hc] = jnp.dot(sk_ref[hc], qt_ref[hc * half:(hc + 1) * half, :],
                            preferred_element_type=F32)

    sub = lax.broadcasted_iota(jnp.int32, (SUBLANES, LANES), 0)

    def sublane_reduce(x, op):
        shift = SUBLANES // 2
        while shift:
            x = op(x, pltpu.roll(x, shift, axis=0))
            shift //= 2
        return x

    def head(h, carry):
        for lt in range(tt // LANES):
            ls = slice(lt * LANES, (lt + 1) * LANES)
            tiles = lambda hc: [s_ref[hc, i * SUBLANES:(i + 1) * SUBLANES, ls] for i in range(k)]
            v1 = _top_sorted(tiles(2 * h), k)
            v2 = _top_sorted(tiles(2 * h + 1), k)
            lo2 = _stack_sublanes(v2[:SUBLANES], sub)
            hi2 = _stack_sublanes(v2[SUBLANES:], sub)
            hi1 = _stack_sublanes(v1[SUBLANES:], sub)
            parts = [(v1[0], lo2), (v1[0], hi2)] + [(v1[r], lo2) for r in range(1, SUBLANES)] \
                + [(hi1, v2[0])]
            cand = [p + q for p, q in parts]
            best = _top_sorted(cand + [None] * (k - len(cand)), k)
            m = best[0]
            tau = best[k - 1]
            picked = [c >= tau for c in cand]
            z = sum(jnp.where(pk, jnp.exp(c - m), 0.0) for pk, c in zip(picked, cand))
            z = sublane_reduce(z, jnp.add)
            shift = m * LOG2E + jnp.log2(z) + 1.0
            dd = [jnp.where(pk, (p * LOG2E - shift) + q * LOG2E, jnp.inf)
                  for pk, (p, q) in zip(picked, parts)]
            th = dd[0]
            for x in dd[1:]:
                th = jnp.minimum(th, x)
            th_ref[h, :, ls] = sublane_reduce(th, jnp.minimum)
            a1_ref[h, :, ls] = s_ref[2 * h, :, ls] * LOG2E - shift[0:1, :]
            b2_ref[h, :, ls] = s_ref[2 * h + 1, :, ls] * LOG2E
        return carry

    lax.fori_loop(0, n_heads, head, 0)


def _peer_query(h2t, wq_t, sk, tt):
    d, t = h2t.shape
    hq = wq_t.shape[0]
    n_hc, nk, half = sk.shape
    n_heads = n_hc // 2
    return pl.pallas_call(
        _peer_query_kernel,
        grid=(t // tt,),
        in_specs=[
            pl.BlockSpec((d, tt), lambda i: (0, i)),
            pl.BlockSpec((hq, d), lambda i: (0, 0)),
            pl.BlockSpec((n_hc, nk, half), lambda i: (0, 0, 0)),
        ],
        out_specs=[
            pl.BlockSpec((n_heads, nk, tt), lambda i: (0, 0, i)),
            pl.BlockSpec((n_heads, nk, tt), lambda i: (0, 0, i)),
            pl.BlockSpec((n_heads, SUBLANES, tt), lambda i: (0, 0, i)),
        ],
        out_shape=[
            jax.ShapeDtypeStruct((n_heads, nk, t), F32),
            jax.ShapeDtypeStruct((n_heads, nk, t), F32),
            jax.ShapeDtypeStruct((n_heads, SUBLANES, t), F32),
        ],
        scratch_shapes=[
            pltpu.VMEM((hq, tt), BF16),
            pltpu.VMEM((n_hc, nk, tt), F32),
        ],
        compiler_params=_params(1),
        name="peer_query",
    )(h2t, wq_t, sk)


GELU_C1 = math.sqrt(2.0 / math.pi)
GELU_C3 = GELU_C1 * 0.044715


def _peer_expert_kernel(h2t_ref, a1_ref, b2_ref, th_ref, u_ref, vt_ref, x2_ref, gf_ref, o_ref,
                        acc_ref, act_ref, w_ref):
    et = pl.program_id(1)
    e_tile = u_ref.shape[0]
    tt = h2t_ref.shape[1]
    n_heads, nk, _ = b2_ref.shape
    n_i1 = e_tile // nk
    pair = 2 * SUBLANES

    @pl.when(et == 0)
    def _():
        acc_ref[...] = jnp.zeros_like(acc_ref)

    act_ref[...] = jnp.dot(u_ref[...], h2t_ref[...], preferred_element_type=F32)

    for lt in range(tt // LANES):
        ls = slice(lt * LANES, (lt + 1) * LANES)
        th = [th_ref[h, :, ls] for h in range(n_heads)]
        for i1 in range(n_i1):
            a = [jnp.broadcast_to(a1_ref[h, i1:i1 + 1, ls], (SUBLANES, LANES)) for h in range(n_heads)]
            for blk in range(nk // pair):
                halves = []
                for half in range(2):
                    r2 = blk * pair + half * SUBLANES
                    g = None
                    for h in range(n_heads):
                        dd = a[h] + b2_ref[h, r2:r2 + SUBLANES, ls]
                        term = jnp.where(dd >= th[h], jnp.exp2(dd), 0.0)
                        g = term if g is None else g + term
                    x = act_ref[i1 * nk + r2:i1 * nk + r2 + SUBLANES, ls]
                    halves.append((g * x) * (1.0 + jnp.tanh(x * (GELU_C1 + GELU_C3 * (x * x)))))
                r = i1 * nk + blk * pair
                w_ref[r:r + pair, ls] = jnp.concatenate(halves, axis=0).astype(BF16)

    acc_ref[...] += jnp.dot(vt_ref[...], w_ref[...], preferred_element_type=F32)

    @pl.when(et == pl.num_programs(1) - 1)
    def _():
        x3 = x2_ref[...] + acc_ref[...].T
        o_ref[...] = _rms(x3, gf_ref[...])


def _peer_expert(h2t, a1, b2, th, u, vt, x2, g_final, tt, e_tile):
    d, t = h2t.shape
    n_heads, nk, _ = a1.shape
    n_exp = u.shape[0]
    n_i1 = e_tile // nk
    assert n_i1 % SUBLANES == 0 or n_i1 == nk
    return pl.pallas_call(
        _peer_expert_kernel,
        grid=(t // tt, n_exp // e_tile),
        in_specs=[
            pl.BlockSpec((d, tt), lambda i, e: (0, i)),
            pl.BlockSpec((n_heads, n_i1, tt), lambda i, e: (0, e, i)),
            pl.BlockSpec((n_heads, nk, tt), lambda i, e: (0, 0, i)),
            pl.BlockSpec((n_heads, SUBLANES, tt), lambda i, e: (0, 0, i)),
            pl.BlockSpec((e_tile, d), lambda i, e: (e, 0)),
            pl.BlockSpec((d, e_tile), lambda i, e: (0, e)),
            pl.BlockSpec((tt, d), lambda i, e: (i, 0)),
            pl.BlockSpec((1, d), lambda i, e: (0, 0)),
        ],
        out_specs=pl.BlockSpec((tt, d), lambda i, e: (i, 0)),
        out_shape=jax.ShapeDtypeStruct((t, d), F32),
        scratch_shapes=[
            pltpu.VMEM((d, tt), F32),
            pltpu.VMEM((e_tile, tt), F32),
            pltpu.VMEM((e_tile, tt), BF16),
        ],
        compiler_params=_params(2),
        name="peer_expert",
    )(h2t, a1, b2, th, u, vt, x2, g_final.reshape(1, d))


IN_PROJ_ROWS, IN_PROJ_COLS = 1024, 1024
CONV_SEQ = 256
SSM_SEQ = 256
OUT_PROJ_ROWS = 256
QUERY_TOKENS = 256
EXPERT_TOKENS = 512
EXPERT_I1_ROWS = SUBLANES


def _tile(n, pref):
    t = min(n, pref)
    assert n % t == 0
    return t


def kernel(x, norm_mix, w_in, b_gate, conv_w_dw, conv_b_dw, conv_ln_g, conv_ln_b, conv_w_out, ssm_a_re, ssm_a_im, ssm_log_dt, ssm_b_re, ssm_b_im, ssm_c_re, ssm_c_im, ssm_d, ssm_w_val, ssm_w_gate, w_out, norm_ffn, peer_w_q, peer_sub_keys, peer_u, peer_v, norm_final):
    bsz, seq, d = x.shape
    depth = w_in.shape[0]
    t = bsz * seq
    cw = conv_w_dw.shape[-1]
    sw = ssm_d.shape[-1]
    n_heads, _, nk, half = peer_sub_keys.shape[1:]
    assert 2 * cw == d and 2 * sw == d and peer_u.shape[1] == nk * nk

    assert depth == 1, "the final rmsnorm is fused into the PEER expert stage"
    l = 0
    x2d = x.reshape(t, d).astype(F32)
    w = w_in[l]
    w_perm = jnp.concatenate([w[:, :2 * cw], w[:, 2 * cw + sw:], w[:, 2 * cw:2 * cw + sw]],
                             axis=1).astype(BF16)
    proj = _in_proj(x2d, norm_mix[l], w_perm, _tile(t, IN_PROJ_ROWS), IN_PROJ_COLS)

    conv_gated = _conv_branch(proj, bsz, seq, conv_w_dw[l], conv_b_dw[l], conv_ln_g[l],
                              conv_ln_b[l], conv_w_out[l].astype(BF16), b_gate[l, :d],
                              _tile(seq, CONV_SEQ))

    lam_t, bmat, cmat = _ssm_tables(ssm_a_re[l], ssm_a_im[l], ssm_log_dt[l], ssm_b_re[l],
                                    ssm_b_im[l], ssm_c_re[l], ssm_c_im[l])
    merged = _ssm_branch(proj, conv_gated, bsz, seq, lam_t, bmat, cmat, ssm_d[l],
                         ssm_w_val[l].astype(BF16), ssm_w_gate[l].astype(BF16), b_gate[l, d:],
                         _tile(seq, SSM_SEQ))

    x2, h2t = _out_proj(merged, x2d, w_out[l].astype(BF16), norm_ffn[l], _tile(t, OUT_PROJ_ROWS))

    a1, b2, th = _peer_query(h2t, peer_w_q[l].T.astype(BF16),
                             peer_sub_keys[l].reshape(2 * n_heads, nk, half).astype(BF16),
                             _tile(t, QUERY_TOKENS))
    out = _peer_expert(h2t, a1, b2, th, peer_u[l].astype(BF16), peer_v[l].astype(BF16).T, x2,
                       norm_final, _tile(t, EXPERT_TOKENS), EXPERT_I1_ROWS * nk)
    return out.reshape(bsz, seq, d).astype(x.dtype)
```

```python
import functools
import math

import jax
import jax.numpy as jnp
from jax import lax
from jax.experimental import pallas as pl
from jax.experimental.pallas import tpu as pltpu

F32 = jnp.float32
BF16 = jnp.bfloat16

RMS_EPS = 1e-6
LN_EPS = 1e-5
PEER_TOPK = 16
LOG2E = 1.0 / math.log(2.0)

SUBLANES = 8
LANES = 128
VMEM_LIMIT_BYTES = 58 * 1024 * 1024


def _params(n_axes):
    return pltpu.CompilerParams(
        dimension_semantics=("arbitrary",) * n_axes,
        vmem_limit_bytes=VMEM_LIMIT_BYTES,
    )


def _rms(x, g):
    ms = jnp.mean(x * x, axis=-1, keepdims=True)
    return x * lax.rsqrt(ms + RMS_EPS) * g


def _in_proj_kernel(x_ref, g_ref, w_ref, o_ref, h_ref):
    @pl.when(pl.program_id(1) == 0)
    def _():
        h_ref[...] = _rms(x_ref[...], g_ref[...]).astype(BF16)

    o_ref[...] = jnp.dot(h_ref[...], w_ref[...], preferred_element_type=F32)


def _in_proj(x2d, g, w, tm, tn):
    t, d = x2d.shape
    n = w.shape[1]
    return pl.pallas_call(
        _in_proj_kernel,
        grid=(t // tm, n // tn),
        in_specs=[
            pl.BlockSpec((tm, d), lambda i, j: (i, 0)),
            pl.BlockSpec((1, d), lambda i, j: (0, 0)),
            pl.BlockSpec((d, tn), lambda i, j: (0, j)),
        ],
        out_specs=pl.BlockSpec((tm, tn), lambda i, j: (i, j)),
        out_shape=jax.ShapeDtypeStruct((t, n), F32),
        scratch_shapes=[pltpu.VMEM((tm, d), BF16)],
        compiler_params=_params(2),
        name="in_proj",
    )(x2d, g.reshape(1, d), w)


CONV_HALO = 32
CONV_ROWS = 32
CONV_COLS = 512


def _conv_kernel(a_ref, b_ref, ah_ref, bh_ref, gate_ref, wdw_ref, bdw_ref, lng_ref, lnb_ref,
                 wpw_ref, bg_ref, o_ref, buf_ref, sh_ref, y_ref):
    ts, c = a_ref.shape
    k_taps = wdw_ref.shape[0]
    n = ts + CONV_HALO
    first = pl.program_id(1) == 0

    glu = a_ref[...] * jax.nn.sigmoid(b_ref[...])
    halo = ah_ref[...] * jax.nn.sigmoid(bh_ref[...])
    buf_ref[0:CONV_HALO, :] = jnp.where(first, 0.0, halo)
    buf_ref[CONV_HALO:n, :] = glu
    buf_ref[n:n + SUBLANES, :] = jnp.zeros((SUBLANES, c), F32)
    for r in range(1, SUBLANES):
        sh_ref[r - 1] = buf_ref[r:r + n, :]

    off = CONV_HALO - (k_taps - 1)

    def rows(rc, carry):
        r0 = pl.multiple_of(rc * CONV_ROWS, CONV_ROWS)
        for lc in range(c // CONV_COLS):
            cols = slice(lc * CONV_COLS, (lc + 1) * CONV_COLS)
            acc = [None] * (CONV_ROWS // SUBLANES)
            for k in range(k_taps):
                q, r = divmod(k + off, SUBLANES)
                wk = wdw_ref[k, :, cols]
                for rt in range(len(acc)):
                    rows_k = pl.ds(r0 + SUBLANES * (q + rt), SUBLANES)
                    win = buf_ref[rows_k, cols] if r == 0 else sh_ref[r - 1, rows_k, cols]
                    acc[rt] = win * wk if acc[rt] is None else acc[rt] + win * wk
            for rt in range(len(acc)):
                y_ref[pl.ds(r0 + SUBLANES * rt, SUBLANES), cols] = acc[rt] + bdw_ref[:, cols]
        return carry

    lax.fori_loop(0, ts // CONV_ROWS, rows, 0)

    y = y_ref[...]
    mu = jnp.mean(y, axis=-1, keepdims=True)
    dlt = y - mu
    var = jnp.mean(dlt * dlt, axis=-1, keepdims=True)
    yn = dlt * lax.rsqrt(var + LN_EPS) * lng_ref[...] + lnb_ref[...]
    z = (yn * jax.nn.sigmoid(yn)).astype(BF16)
    out = jnp.dot(z, wpw_ref[...], preferred_element_type=F32)
    o_ref[...] = jax.nn.sigmoid(gate_ref[...] + bg_ref[...]) * out


def _conv_branch(proj, bsz, seq, w_dw, b_dw, ln_g, ln_b, w_pw, b_gate0, ts):
    t = proj.shape[0]
    k_taps, c = w_dw.shape
    d = w_pw.shape[1]
    ns = seq // ts
    hb = ts // CONV_HALO
    assert d == 2 * c and k_taps - 1 <= CONV_HALO and ts % CONV_ROWS == 0 and seq % ts == 0

    def halo_idx(col):
        return lambda b, i: (jnp.maximum((b * ns + i) * hb - 1, 0), col)

    row = lambda a: a.reshape(1, -1)
    return pl.pallas_call(
        _conv_kernel,
        grid=(bsz, ns),
        in_specs=[
            pl.BlockSpec((ts, c), lambda b, i: (b * ns + i, 0)),
            pl.BlockSpec((ts, c), lambda b, i: (b * ns + i, 1)),
            pl.BlockSpec((CONV_HALO, c), halo_idx(0)),
            pl.BlockSpec((CONV_HALO, c), halo_idx(1)),
            pl.BlockSpec((ts, d), lambda b, i: (b * ns + i, 1)),
            pl.BlockSpec((k_taps, SUBLANES, c), lambda b, i: (0, 0, 0)),
            pl.BlockSpec((1, c), lambda b, i: (0, 0)),
            pl.BlockSpec((1, c), lambda b, i: (0, 0)),
            pl.BlockSpec((1, c), lambda b, i: (0, 0)),
            pl.BlockSpec((c, d), lambda b, i: (0, 0)),
            pl.BlockSpec((1, d), lambda b, i: (0, 0)),
        ],
        out_specs=pl.BlockSpec((ts, d), lambda b, i: (b * ns + i, 0)),
        out_shape=jax.ShapeDtypeStruct((t, d), F32),
        scratch_shapes=[
            pltpu.VMEM((ts + CONV_HALO + SUBLANES, c), F32),
            pltpu.VMEM((SUBLANES - 1, ts + CONV_HALO, c), F32),
            pltpu.VMEM((ts, c), F32),
        ],
        compiler_params=_params(2),
        name="conv_branch",
    )(proj, proj, proj, proj, proj, jnp.broadcast_to(w_dw[:, None, :], (k_taps, SUBLANES, c)),
      row(b_dw), row(ln_g), row(ln_b), w_pw, row(b_gate0))


def _ssm_kernel(u_ref, cg_ref, gate_ref, bmat_ref, lam_ref, cmat_ref, dskip_ref, wval_ref,
                wgate_ref, bg_ref, o_ref, st_ref, carry_ref):
    sl, w = u_ref.shape
    n_tiles = w // LANES
    hw = n_tiles // 2

    @pl.when(pl.program_id(1) == 0)
    def _():
        carry_ref[...] = jnp.zeros_like(carry_ref)

    for j in range(SUBLANES):
        bu = jnp.dot(u_ref[:, j * LANES:(j + 1) * LANES].astype(BF16), bmat_ref[j],
                     preferred_element_type=F32)
        for lt in range(n_tiles):
            st_ref[lt, pl.ds(j, sl, stride=SUBLANES), :] = bu[:, lt * LANES:(lt + 1) * LANES]

    lam = [lam_ref[:, lt * LANES:(lt + 1) * LANES] for lt in range(n_tiles)]

    def step(t, s):
        r = pl.multiple_of(t * SUBLANES, SUBLANES)
        new = [None] * n_tiles
        for q in range(hw):
            sr, si, ar, ai = s[q], s[q + hw], lam[q], lam[q + hw]
            new[q] = ar * sr - ai * si + st_ref[q, pl.ds(r, SUBLANES), :]
            new[q + hw] = ar * si + ai * sr + st_ref[q + hw, pl.ds(r, SUBLANES), :]
        for lt in range(n_tiles):
            st_ref[lt, pl.ds(r, SUBLANES), :] = new[lt]
        return tuple(new)

    s_end = lax.fori_loop(0, sl, step, tuple(carry_ref[lt] for lt in range(n_tiles)), unroll=4)
    for lt in range(n_tiles):
        carry_ref[lt] = s_end[lt]

    ys = []
    for j in range(SUBLANES):
        sj = jnp.concatenate([st_ref[lt, pl.ds(j, sl, stride=SUBLANES), :] for lt in range(n_tiles)],
                             axis=1)
        ys.append(jnp.dot(sj.astype(BF16), cmat_ref[j], preferred_element_type=F32))
    y = jnp.concatenate(ys, axis=1)

    y = y + dskip_ref[...] * u_ref[...]
    z = jax.nn.gelu(y).astype(BF16)
    val = jnp.dot(z, wval_ref[...], preferred_element_type=F32)
    gt = jnp.dot(z, wgate_ref[...], preferred_element_type=F32)
    branch = val * jax.nn.sigmoid(gt)
    o_ref[...] = cg_ref[...] + jax.nn.sigmoid(gate_ref[...] + bg_ref[...]) * branch


def _ssm_discretize_kernel(are_ref, aim_ref, ldt_ref, bre_ref, bim_ref, lre_ref, lim_ref, obr_ref, obi_ref):
    a_re = are_ref[...]
    a_im = aim_ref[...]
    dt = jnp.exp(ldt_ref[...])
    mag = jnp.exp(a_re * dt)
    l_re = mag * jnp.cos(a_im * dt)
    l_im = mag * jnp.sin(a_im * dt)
    den = a_re * a_re + a_im * a_im
    c_re = ((l_re - 1.0) * a_re + l_im * a_im) / den
    c_im = (l_im * a_re - (l_re - 1.0) * a_im) / den
    lre_ref[...] = l_re
    lim_ref[...] = l_im
    obr_ref[...] = c_re * bre_ref[...] - c_im * bim_ref[...]
    obi_ref[...] = c_re * bim_ref[...] + c_im * bre_ref[...]


def _ssm_tables(a_re, a_im, log_dt, b_re, b_im, c_re, c_im):
    g, p = a_re.shape
    gs = b_re.shape[-1]
    assert g % SUBLANES == 0
    gl = g // SUBLANES
    rows = lambda x: jnp.repeat(x.astype(F32), gs, axis=0)
    flat = lambda x: jnp.swapaxes(x.astype(F32), 1, 2).reshape(g * gs, p)
    shape = jax.ShapeDtypeStruct((g * gs, p), F32)
    l_re, l_im, bb_re, bb_im = pl.pallas_call(
        _ssm_discretize_kernel, out_shape=[shape] * 4, name="ssm_discretize",
    )(rows(a_re), rows(a_im), rows(jnp.broadcast_to(log_dt[:, None], (g, p))), flat(b_re), flat(b_im))

    pick = lambda x: x.reshape(g, gs, p)[:, 0, :].reshape(SUBLANES, gl * p)
    lam_t = jnp.concatenate([pick(l_re), pick(l_im)], axis=1)
    eye = jnp.eye(gl, dtype=F32)

    def bmap(x):
        x = x.reshape(SUBLANES, gl, gs, p)
        return jnp.einsum('jghp,gk->jghkp', x, eye).reshape(SUBLANES, gl * gs, gl * p)

    bmat = jnp.concatenate([bmap(bb_re), bmap(bb_im)], axis=2)

    def cmap(x):
        x = x.astype(F32).reshape(SUBLANES, gl, gs, p)
        return jnp.einsum('cghp,gk->cgpkh', x, eye).reshape(SUBLANES, gl * p, gl * gs)

    cmat = jnp.concatenate([cmap(c_re), -cmap(c_im)], axis=1)
    return lam_t, bmat.astype(BF16), cmat.astype(BF16)


def _ssm_branch(proj, conv_gated, bsz, seq, lam_t, bmat, cmat, d_skip, w_val, w_gate, b_gate1, sl):
    t = proj.shape[0]
    w = d_skip.shape[0]
    d = w_val.shape[1]
    ns = seq // sl
    n_tiles = w // LANES
    assert n_tiles == SUBLANES and d == 2 * w and seq % sl == 0
    u_col = proj.shape[1] // w - 1
    row = lambda a: a.reshape(1, -1)
    const = lambda b, i: (0, 0)
    const3 = lambda b, i: (0, 0, 0)
    return pl.pallas_call(
        _ssm_kernel,
        grid=(bsz, ns),
        in_specs=[
            pl.BlockSpec((sl, w), lambda b, i: (b * ns + i, u_col)),
            pl.BlockSpec((sl, d), lambda b, i: (b * ns + i, 0)),
            pl.BlockSpec((sl, d), lambda b, i: (b * ns + i, 2)),
            pl.BlockSpec((SUBLANES, LANES, w), const3),
            pl.BlockSpec((SUBLANES, w), const),
            pl.BlockSpec((SUBLANES, w, LANES), const3),
            pl.BlockSpec((1, w), const),
            pl.BlockSpec((w, d), const),
            pl.BlockSpec((w, d), const),
            pl.BlockSpec((1, d), const),
        ],
        out_specs=pl.BlockSpec((sl, d), lambda b, i: (b * ns + i, 0)),
        out_shape=jax.ShapeDtypeStruct((t, d), F32),
        scratch_shapes=[
            pltpu.VMEM((n_tiles, sl * SUBLANES, LANES), F32),
            pltpu.VMEM((n_tiles, SUBLANES, LANES), F32),
        ],
        compiler_params=_params(2),
        name="ssm_branch",
    )(proj, conv_gated, proj, bmat, lam_t, cmat, row(d_skip), w_val, w_gate, row(b_gate1))


def _out_proj_kernel(m_ref, x_ref, w_ref, g_ref, x2_ref, h2t_ref):
    x2 = x_ref[...] + jnp.dot(m_ref[...].astype(BF16), w_ref[...], preferred_element_type=F32)
    x2_ref[...] = x2
    h2t_ref[...] = _rms(x2, g_ref[...]).T.astype(BF16)


def _out_proj(merged, x2d, w, g, tm):
    t, d = x2d.shape
    return pl.pallas_call(
        _out_proj_kernel,
        grid=(t // tm,),
        in_specs=[
            pl.BlockSpec((tm, d), lambda i: (i, 0)),
            pl.BlockSpec((tm, d), lambda i: (i, 0)),
            pl.BlockSpec((d, d), lambda i: (0, 0)),
            pl.BlockSpec((1, d), lambda i: (0, 0)),
        ],
        out_specs=[pl.BlockSpec((tm, d), lambda i: (i, 0)), pl.BlockSpec((d, tm), lambda i: (0, i))],
        out_shape=[jax.ShapeDtypeStruct((t, d), F32), jax.ShapeDtypeStruct((d, t), BF16)],
        compiler_params=_params(1),
        name="out_proj",
    )(merged, x2d, w, g.reshape(1, d))


def _sort_pairs(n):
    pairs = []

    def merge(lo, cnt, r):
        step = 2 * r
        if step < cnt:
            merge(lo, cnt, step)
            merge(lo + r, cnt, step)
            pairs.extend((i, i + r) for i in range(lo + r, lo + cnt - r, step))
        else:
            pairs.append((lo, lo + r))

    def sort(lo, cnt):
        if cnt > 1:
            sort(lo, cnt // 2)
            sort(lo + cnt // 2, cnt // 2)
            merge(lo, cnt, 1)

    sort(0, n)
    return pairs


def _bitonic_pairs(n):
    pairs = []
    stride = n // 2
    while stride:
        pairs.extend((i, i + stride) for i in range(n) if not i & stride)
        stride //= 2
    return pairs


def _apply_pairs(v, pairs):
    v = list(v)
    for i, j in pairs:
        if v[j] is None:
            continue
        if v[i] is None:
            v[i], v[j] = v[j], None
        else:
            v[i], v[j] = jnp.maximum(v[i], v[j]), jnp.minimum(v[i], v[j])
    return v


def _top_sorted(v, k):
    v = _apply_pairs(v, _sort_pairs(k))
    shift = SUBLANES // 2
    while shift:
        w = [None if x is None else pltpu.roll(x, shift, axis=0) for x in v]
        merged = []
        for i in range(k):
            x, y = v[i], w[k - 1 - i]
            merged.append(y if x is None else x if y is None else jnp.maximum(x, y))
        v = _apply_pairs(merged, _bitonic_pairs(k))
        shift //= 2
    return v


def _stack_sublanes(rows, sub):
    out = rows[0]
    for j in range(1, len(rows)):
        out = jnp.where(sub == j, rows[j], out)
    return out


def _peer_query_kernel(h2t_ref, wq_ref, sk_ref, a1_ref, b2_ref, th_ref, qt_ref, s_ref):
    tt = h2t_ref.shape[1]
    n_hc, nk, half = sk_ref.shape
    n_heads = n_hc // 2
    k = PEER_TOPK
    assert k == 2 * SUBLANES and nk == k * SUBLANES

    qt_ref[...] = jnp.dot(wq_ref[...], h2t_ref[...], preferred_element_type=F32).astype(BF16)
    for hc in range(n_hc):
        s_ref[hc] = jnp.dot(sk_ref[hc], qt_ref[hc * half:(hc + 1) * half, :],
                            preferred_element_type=F32)

    sub = lax.broadcasted_iota(jnp.int32, (SUBLANES, LANES), 0)

    def sublane_reduce(x, op):
        shift = SUBLANES // 2
        while shift:
            x = op(x, pltpu.roll(x, shift, axis=0))
            shift //= 2
        return x

    def head(h, carry):
        for lt in range(tt // LANES):
            ls = slice(lt * LANES, (lt + 1) * LANES)
            tiles = lambda hc: [s_ref[hc, i * SUBLANES:(i + 1) * SUBLANES, ls] for i in range(k)]
            v1 = _top_sorted(tiles(2 * h), k)
            v2 = _top_sorted(tiles(2 * h + 1), k)
            lo2 = _stack_sublanes(v2[:SUBLANES], sub)
            hi2 = _stack_sublanes(v2[SUBLANES:], sub)
            hi1 = _stack_sublanes(v1[SUBLANES:], sub)
            parts = [(v1[0], lo2), (v1[0], hi2)] + [(v1[r], lo2) for r in range(1, SUBLANES)] \
                + [(hi1, v2[0])]
            cand = [p + q for p, q in parts]
            best = _top_sorted(cand + [None] * (k - len(cand)), k)
            m = best[0]
            tau = best[k - 1]
            picked = [c >= tau for c in cand]
            z = sum(jnp.where(pk, jnp.exp(c - m), 0.0) for pk, c in zip(picked, cand))
            z = sublane_reduce(z, jnp.add)
            shift = m * LOG2E + jnp.log2(z) + 1.0
            dd = [jnp.where(pk, (p * LOG2E - shift) + q * LOG2E, jnp.inf)
                  for pk, (p, q) in zip(picked, parts)]
            th = dd[0]
            for x in dd[1:]:
                th = jnp.minimum(th, x)
            th_ref[h, :, ls] = sublane_reduce(th, jnp.minimum)
            a1_ref[h, :, ls] = s_ref[2 * h, :, ls] * LOG2E - shift[0:1, :]
            b2_ref[h, :, ls] = s_ref[2 * h + 1, :, ls] * LOG2E
        return carry

    lax.fori_loop(0, n_heads, head, 0)


def _peer_query(h2t, wq_t, sk, tt):
    d, t = h2t.shape
    hq = wq_t.shape[0]
    n_hc, nk, half = sk.shape
    n_heads = n_hc // 2
    return pl.pallas_call(
        _peer_query_kernel,
        grid=(t // tt,),
        in_specs=[
            pl.BlockSpec((d, tt), lambda i: (0, i)),
            pl.BlockSpec((hq, d), lambda i: (0, 0)),
            pl.BlockSpec((n_hc, nk, half), lambda i: (0, 0, 0)),
        ],
        out_specs=[
            pl.BlockSpec((n_heads, nk, tt), lambda i: (0, 0, i)),
            pl.BlockSpec((n_heads, nk, tt), lambda i: (0, 0, i)),
            pl.BlockSpec((n_heads, SUBLANES, tt), lambda i: (0, 0, i)),
        ],
        out_shape=[
            jax.ShapeDtypeStruct((n_heads, nk, t), F32),
            jax.ShapeDtypeStruct((n_heads, nk, t), F32),
            jax.ShapeDtypeStruct((n_heads, SUBLANES, t), F32),
        ],
        scratch_shapes=[
            pltpu.VMEM((hq, tt), BF16),
            pltpu.VMEM((n_hc, nk, tt), F32),
        ],
        compiler_params=_params(1),
        name="peer_query",
    )(h2t, wq_t, sk)


GELU_C1 = math.sqrt(2.0 / math.pi)
GELU_C3 = GELU_C1 * 0.044715


def _zero_after(x):
    bits = pltpu.bitcast(x, jnp.uint32)
    bits = lax.shift_right_logical(lax.shift_right_logical(bits, jnp.uint32(16)), jnp.uint32(16))
    return pltpu.bitcast(bits, F32)


def _expert_weights(act_ref, w_ref, a1_ref, b2_ref, th_ref, i1_row0, anchors):
    e_tile, tt = act_ref.shape
    n_heads, nk, _ = b2_ref.shape
    pair = 2 * SUBLANES
    blocks = [(lt, i1) for lt in range(tt // LANES) for i1 in range(e_tile // nk)]
    for (lt, i1), anchor in zip(blocks, anchors):
        ls = slice(lt * LANES, (lt + 1) * LANES)
        zero = _zero_after(anchor)
        th = [th_ref[h, :, ls] + zero for h in range(n_heads)]
        a = [jnp.broadcast_to(a1_ref[h, i1_row0 + i1:i1_row0 + i1 + 1, ls], (SUBLANES, LANES))
             for h in range(n_heads)]
        for blk in range(nk // pair):
            halves = []
            for half in range(2):
                r2 = blk * pair + half * SUBLANES
                g = None
                for h in range(n_heads):
                    dd = a[h] + b2_ref[h, r2:r2 + SUBLANES, ls]
                    term = jnp.where(dd >= th[h], jnp.exp2(dd), 0.0)
                    g = term if g is None else g + term
                x = act_ref[i1 * nk + r2:i1 * nk + r2 + SUBLANES, ls]
                halves.append((g * x) * (1.0 + jnp.tanh(x * (GELU_C1 + GELU_C3 * (x * x)))))
            r = i1 * nk + blk * pair
            w_ref[r:r + pair, ls] = jnp.concatenate(halves, axis=0).astype(BF16)


def _peer_expert_kernel(h2t_ref, a1p_ref, b2p_ref, thp_ref, a1c_ref, b2c_ref, thc_ref, u_ref, vt_ref,
                        x2_ref, gf_ref, o_ref, acc_ref, act0_ref, act1_ref, w0_ref, w1_ref, *, n_e):
    g = pl.program_id(0)
    e_tile, tt = act0_ref.shape
    nk = b2c_ref.shape[1]
    n_i1 = e_tile // nk

    @pl.when(g == 0)
    def _():
        acc_ref[...] = jnp.zeros_like(acc_ref)
        act1_ref[...] = jnp.zeros_like(act1_ref)
        w0_ref[...] = jnp.zeros_like(w0_ref)

    def tick(u_rows, vt_cols, act_new, act_old, w_old, w_new, a1_ref, b2_ref, th_ref, i1_row0):
        half = (tt // LANES) * n_i1 // 2
        new_act = jnp.dot(u_ref[u_rows, :], h2t_ref[...], preferred_element_type=F32)
        upd = jnp.dot(vt_ref[:, vt_cols], w_old[...], preferred_element_type=F32)
        anchors = [new_act[0:SUBLANES, 0:LANES]] * half
        anchors += [new_act[0:SUBLANES, tt // 2:tt // 2 + LANES]] * half
        _expert_weights(act_old, w_new, a1_ref, b2_ref, th_ref, i1_row0, anchors)
        act_new[...] = new_act
        acc_ref[...] += upd

    tick(slice(0, e_tile), slice(0, e_tile), act0_ref, act1_ref, w0_ref, w1_ref,
         a1p_ref, b2p_ref, thp_ref, n_i1)
    tick(slice(e_tile, 2 * e_tile), slice(e_tile, 2 * e_tile), act1_ref, act0_ref, w1_ref, w0_ref,
         a1c_ref, b2c_ref, thc_ref, 0)

    @pl.when(jnp.logical_and(g > 0, (2 * g) % n_e == 0))
    def _():
        x3 = x2_ref[...] + acc_ref[...].T
        o_ref[...] = _rms(x3, gf_ref[...])
        acc_ref[...] = jnp.zeros_like(acc_ref)


def _peer_expert(h2t, a1, b2, th, u, vt, x2, g_final, tt, e_tile):
    d, t = h2t.shape
    n_heads, nk, _ = a1.shape
    n_exp = u.shape[0]
    n_t = t // tt
    n_e = n_exp // e_tile
    assert n_e % 2 == 0 and 2 * e_tile == SUBLANES * nk
    half = n_e // 2
    steps = n_t * half + 1
    once = pl.Buffered(1)

    tok_a = lambda g: jnp.minimum((2 * g) // n_e, n_t - 1)
    tok_p = lambda g: jnp.maximum(2 * g - 1, 0) // n_e
    tok_c = lambda g: jnp.maximum(2 * g - 2, 0) // n_e
    meta = lambda shape, tok: pl.BlockSpec(shape, lambda g: (0, 0, tok(g)), pipeline_mode=once)
    a1_spec = lambda tick, tok: pl.BlockSpec((n_heads, SUBLANES, tt),
                                             lambda g: (0, (tick(g) % n_e) // 2, tok(g)))
    return pl.pallas_call(
        functools.partial(_peer_expert_kernel, n_e=n_e),
        grid=(steps,),
        in_specs=[
            pl.BlockSpec((d, tt), lambda g: (0, tok_a(g)), pipeline_mode=once),
            a1_spec(lambda g: 2 * g - 1, tok_p), meta((n_heads, nk, tt), tok_p), meta((n_heads, SUBLANES, tt), tok_p),
            a1_spec(lambda g: 2 * g, tok_a), meta((n_heads, nk, tt), tok_a), meta((n_heads, SUBLANES, tt), tok_a),
            pl.BlockSpec((2 * e_tile, d), lambda g: (g % half, 0)),
            pl.BlockSpec((d, 2 * e_tile), lambda g: (0, jnp.maximum(g - 1, 0) % half)),
            pl.BlockSpec((tt, d), lambda g: (tok_c(g), 0), pipeline_mode=once),
            pl.BlockSpec((1, d), lambda g: (0, 0)),
        ],
        out_specs=pl.BlockSpec((tt, d), lambda g: (tok_c(g), 0)),
        out_shape=jax.ShapeDtypeStruct((t, d), F32),
        scratch_shapes=[
            pltpu.VMEM((d, tt), F32),
            pltpu.VMEM((e_tile, tt), F32),
            pltpu.VMEM((e_tile, tt), F32),
            pltpu.VMEM((e_tile, tt), BF16),
            pltpu.VMEM((e_tile, tt), BF16),
        ],
        compiler_params=_params(1),
        name="peer_expert",
    )(h2t, a1, b2, th, a1, b2, th, u, vt, x2, g_final.reshape(1, d))


IN_PROJ_ROWS, IN_PROJ_COLS = 1024, 1024
CONV_SEQ = 256
SSM_SEQ = 256
OUT_PROJ_ROWS = 256
QUERY_TOKENS = 256
EXPERT_TOKENS = 512
EXPERT_I1_ROWS = SUBLANES // 2


def _tile(n, pref):
    t = min(n, pref)
    assert n % t == 0
    return t


def kernel(x, norm_mix, w_in, b_gate, conv_w_dw, conv_b_dw, conv_ln_g, conv_ln_b, conv_w_out, ssm_a_re, ssm_a_im, ssm_log_dt, ssm_b_re, ssm_b_im, ssm_c_re, ssm_c_im, ssm_d, ssm_w_val, ssm_w_gate, w_out, norm_ffn, peer_w_q, peer_sub_keys, peer_u, peer_v, norm_final):
    bsz, seq, d = x.shape
    depth = w_in.shape[0]
    t = bsz * seq
    cw = conv_w_dw.shape[-1]
    sw = ssm_d.shape[-1]
    n_heads, _, nk, half = peer_sub_keys.shape[1:]
    assert 2 * cw == d and 2 * sw == d and peer_u.shape[1] == nk * nk

    assert depth == 1, "the final rmsnorm is fused into the PEER expert stage"
    l = 0
    x2d = x.reshape(t, d).astype(F32)
    w = w_in[l]
    w_perm = jnp.concatenate([w[:, :2 * cw], w[:, 2 * cw + sw:], w[:, 2 * cw:2 * cw + sw]],
                             axis=1).astype(BF16)
    proj = _in_proj(x2d, norm_mix[l], w_perm, _tile(t, IN_PROJ_ROWS), IN_PROJ_COLS)

    conv_gated = _conv_branch(proj, bsz, seq, conv_w_dw[l], conv_b_dw[l], conv_ln_g[l],
                              conv_ln_b[l], conv_w_out[l].astype(BF16), b_gate[l, :d],
                              _tile(seq, CONV_SEQ))

    lam_t, bmat, cmat = _ssm_tables(ssm_a_re[l], ssm_a_im[l], ssm_log_dt[l], ssm_b_re[l],
                                    ssm_b_im[l], ssm_c_re[l], ssm_c_im[l])
    merged = _ssm_branch(proj, conv_gated, bsz, seq, lam_t, bmat, cmat, ssm_d[l],
                         ssm_w_val[l].astype(BF16), ssm_w_gate[l].astype(BF16), b_gate[l, d:],
                         _tile(seq, SSM_SEQ))

    x2, h2t = _out_proj(merged, x2d, w_out[l].astype(BF16), norm_ffn[l], _tile(t, OUT_PROJ_ROWS))

    a1, b2, th = _peer_query(h2t, peer_w_q[l].T.astype(BF16),
                             peer_sub_keys[l].reshape(2 * n_heads, nk, half).astype(BF16),
                             _tile(t, QUERY_TOKENS))
    out = _peer_expert(h2t, a1, b2, th, peer_u[l].astype(BF16), peer_v[l].astype(BF16).T, x2,
                       norm_final, _tile(t, EXPERT_TOKENS), EXPERT_I1_ROWS * nk)
    return out.reshape(bsz, seq, d).astype(x.dtype)
```

```python
import functools
import math

import jax
import jax.numpy as jnp
from jax import lax
from jax.experimental import pallas as pl
from jax.experimental.pallas import tpu as pltpu

F32 = jnp.float32
BF16 = jnp.bfloat16

RMS_EPS = 1e-6
LN_EPS = 1e-5
PEER_TOPK = 16
LOG2E = 1.0 / math.log(2.0)

SUBLANES = 8
LANES = 128
VMEM_LIMIT_BYTES = 58 * 1024 * 1024


def _params(n_axes):
    return pltpu.CompilerParams(
        dimension_semantics=("arbitrary",) * n_axes,
        vmem_limit_bytes=VMEM_LIMIT_BYTES,
    )


def _rms(x, g):
    ms = jnp.mean(x * x, axis=-1, keepdims=True)
    return x * lax.rsqrt(ms + RMS_EPS) * g


def _in_proj_kernel(x_ref, g_ref, w_ref, o_ref, h_ref):
    @pl.when(pl.program_id(1) == 0)
    def _():
        h_ref[...] = _rms(x_ref[...], g_ref[...]).astype(BF16)

    o_ref[...] = jnp.dot(h_ref[...], w_ref[...], preferred_element_type=F32)


def _in_proj(x2d, g, w, tm, tn):
    t, d = x2d.shape
    n = w.shape[1]
    return pl.pallas_call(
        _in_proj_kernel,
        grid=(t // tm, n // tn),
        in_specs=[
            pl.BlockSpec((tm, d), lambda i, j: (i, 0)),
            pl.BlockSpec((1, d), lambda i, j: (0, 0)),
            pl.BlockSpec((d, tn), lambda i, j: (0, j)),
        ],
        out_specs=pl.BlockSpec((tm, tn), lambda i, j: (i, j)),
        out_shape=jax.ShapeDtypeStruct((t, n), F32),
        scratch_shapes=[pltpu.VMEM((tm, d), BF16)],
        compiler_params=_params(2),
        name="in_proj",
    )(x2d, g.reshape(1, d), w)


CONV_HALO = 32
CONV_ROWS = 32
CONV_COLS = 512


def _conv_kernel(a_ref, b_ref, ah_ref, bh_ref, gate_ref, wdw_ref, bdw_ref, lng_ref, lnb_ref,
                 wpw_ref, bg_ref, o_ref, buf_ref, sh_ref, y_ref):
    ts, c = a_ref.shape
    k_taps = wdw_ref.shape[0]
    n = ts + CONV_HALO
    first = pl.program_id(1) == 0

    glu = a_ref[...] * jax.nn.sigmoid(b_ref[...])
    halo = ah_ref[...] * jax.nn.sigmoid(bh_ref[...])
    buf_ref[0:CONV_HALO, :] = jnp.where(first, 0.0, halo)
    buf_ref[CONV_HALO:n, :] = glu
    buf_ref[n:n + SUBLANES, :] = jnp.zeros((SUBLANES, c), F32)
    for r in range(1, SUBLANES):
        sh_ref[r - 1] = buf_ref[r:r + n, :]

    off = CONV_HALO - (k_taps - 1)

    def rows(rc, carry):
        r0 = pl.multiple_of(rc * CONV_ROWS, CONV_ROWS)
        for lc in range(c // CONV_COLS):
            cols = slice(lc * CONV_COLS, (lc + 1) * CONV_COLS)
            acc = [None] * (CONV_ROWS // SUBLANES)
            for k in range(k_taps):
                q, r = divmod(k + off, SUBLANES)
                wk = wdw_ref[k, :, cols]
                for rt in range(len(acc)):
                    rows_k = pl.ds(r0 + SUBLANES * (q + rt), SUBLANES)
                    win = buf_ref[rows_k, cols] if r == 0 else sh_ref[r - 1, rows_k, cols]
                    acc[rt] = win * wk if acc[rt] is None else acc[rt] + win * wk
            for rt in range(len(acc)):
                y_ref[pl.ds(r0 + SUBLANES * rt, SUBLANES), cols] = acc[rt] + bdw_ref[:, cols]
        return carry

    lax.fori_loop(0, ts // CONV_ROWS, rows, 0)

    y = y_ref[...]
    mu = jnp.mean(y, axis=-1, keepdims=True)
    dlt = y - mu
    var = jnp.mean(dlt * dlt, axis=-1, keepdims=True)
    yn = dlt * lax.rsqrt(var + LN_EPS) * lng_ref[...] + lnb_ref[...]
    z = (yn * jax.nn.sigmoid(yn)).astype(BF16)
    out = jnp.dot(z, wpw_ref[...], preferred_element_type=F32)
    o_ref[...] = jax.nn.sigmoid(gate_ref[...] + bg_ref[...]) * out


def _conv_branch(proj, bsz, seq, w_dw, b_dw, ln_g, ln_b, w_pw, b_gate0, ts):
    t = proj.shape[0]
    k_taps, c = w_dw.shape
    d = w_pw.shape[1]
    ns = seq // ts
    hb = ts // CONV_HALO
    assert d == 2 * c and k_taps - 1 <= CONV_HALO and ts % CONV_ROWS == 0 and seq % ts == 0

    def halo_idx(col):
        return lambda b, i: (jnp.maximum((b * ns + i) * hb - 1, 0), col)

    row = lambda a: a.reshape(1, -1)
    return pl.pallas_call(
        _conv_kernel,
        grid=(bsz, ns),
        in_specs=[
            pl.BlockSpec((ts, c), lambda b, i: (b * ns + i, 0)),
            pl.BlockSpec((ts, c), lambda b, i: (b * ns + i, 1)),
            pl.BlockSpec((CONV_HALO, c), halo_idx(0)),
            pl.BlockSpec((CONV_HALO, c), halo_idx(1)),
            pl.BlockSpec((ts, d), lambda b, i: (b * ns + i, 1)),
            pl.BlockSpec((k_taps, SUBLANES, c), lambda b, i: (0, 0, 0)),
            pl.BlockSpec((1, c), lambda b, i: (0, 0)),
            pl.BlockSpec((1, c), lambda b, i: (0, 0)),
            pl.BlockSpec((1, c), lambda b, i: (0, 0)),
            pl.BlockSpec((c, d), lambda b, i: (0, 0)),
            pl.BlockSpec((1, d), lambda b, i: (0, 0)),
        ],
        out_specs=pl.BlockSpec((ts, d), lambda b, i: (b * ns + i, 0)),
        out_shape=jax.ShapeDtypeStruct((t, d), F32),
        scratch_shapes=[
            pltpu.VMEM((ts + CONV_HALO + SUBLANES, c), F32),
            pltpu.VMEM((SUBLANES - 1, ts + CONV_HALO, c), F32),
            pltpu.VMEM((ts, c), F32),
        ],
        compiler_params=_params(2),
        name="conv_branch",
    )(proj, proj, proj, proj, proj, jnp.broadcast_to(w_dw[:, None, :], (k_taps, SUBLANES, c)),
      row(b_dw), row(ln_g), row(ln_b), w_pw, row(b_gate0))


def _ssm_kernel(u_ref, cg_ref, gate_ref, bmat_ref, lam_ref, cmat_ref, dskip_ref, wval_ref,
                wgate_ref, bg_ref, o_ref, st_ref, carry_ref):
    sl, w = u_ref.shape
    n_tiles = w // LANES
    hw = n_tiles // 2

    @pl.when(pl.program_id(1) == 0)
    def _():
        carry_ref[...] = jnp.zeros_like(carry_ref)

    for j in range(SUBLANES):
        bu = jnp.dot(u_ref[:, j * LANES:(j + 1) * LANES].astype(BF16), bmat_ref[j],
                     preferred_element_type=F32)
        for lt in range(n_tiles):
            st_ref[lt, pl.ds(j, sl, stride=SUBLANES), :] = bu[:, lt * LANES:(lt + 1) * LANES]

    lam = [lam_ref[:, lt * LANES:(lt + 1) * LANES] for lt in range(n_tiles)]

    def step(t, s):
        r = pl.multiple_of(t * SUBLANES, SUBLANES)
        new = [None] * n_tiles
        for q in range(hw):
            sr, si, ar, ai = s[q], s[q + hw], lam[q], lam[q + hw]
            new[q] = ar * sr - ai * si + st_ref[q, pl.ds(r, SUBLANES), :]
            new[q + hw] = ar * si + ai * sr + st_ref[q + hw, pl.ds(r, SUBLANES), :]
        for lt in range(n_tiles):
            st_ref[lt, pl.ds(r, SUBLANES), :] = new[lt]
        return tuple(new)

    s_end = lax.fori_loop(0, sl, step, tuple(carry_ref[lt] for lt in range(n_tiles)), unroll=4)
    for lt in range(n_tiles):
        carry_ref[lt] = s_end[lt]

    ys = []
    for j in range(SUBLANES):
        sj = jnp.concatenate([st_ref[lt, pl.ds(j, sl, stride=SUBLANES), :] for lt in range(n_tiles)],
                             axis=1)
        ys.append(jnp.dot(sj.astype(BF16), cmat_ref[j], preferred_element_type=F32))
    y = jnp.concatenate(ys, axis=1)

    y = y + dskip_ref[...] * u_ref[...]
    z = jax.nn.gelu(y).astype(BF16)
    val = jnp.dot(z, wval_ref[...], preferred_element_type=F32)
    gt = jnp.dot(z, wgate_ref[...], preferred_element_type=F32)
    branch = val * jax.nn.sigmoid(gt)
    o_ref[...] = cg_ref[...] + jax.nn.sigmoid(gate_ref[...] + bg_ref[...]) * branch


def _ssm_discretize_kernel(are_ref, aim_ref, ldt_ref, bre_ref, bim_ref, lre_ref, lim_ref, obr_ref, obi_ref):
    a_re = are_ref[...]
    a_im = aim_ref[...]
    dt = jnp.exp(ldt_ref[...])
    mag = jnp.exp(a_re * dt)
    l_re = mag * jnp.cos(a_im * dt)
    l_im = mag * jnp.sin(a_im * dt)
    den = a_re * a_re + a_im * a_im
    c_re = ((l_re - 1.0) * a_re + l_im * a_im) / den
    c_im = (l_im * a_re - (l_re - 1.0) * a_im) / den
    lre_ref[...] = l_re
    lim_ref[...] = l_im
    obr_ref[...] = c_re * bre_ref[...] - c_im * bim_ref[...]
    obi_ref[...] = c_re * bim_ref[...] + c_im * bre_ref[...]


def _ssm_tables(a_re, a_im, log_dt, b_re, b_im, c_re, c_im):
    g, p = a_re.shape
    gs = b_re.shape[-1]
    assert g % SUBLANES == 0
    gl = g // SUBLANES
    rows = lambda x: jnp.repeat(x.astype(F32), gs, axis=0)
    flat = lambda x: jnp.swapaxes(x.astype(F32), 1, 2).reshape(g * gs, p)
    shape = jax.ShapeDtypeStruct((g * gs, p), F32)
    l_re, l_im, bb_re, bb_im = pl.pallas_call(
        _ssm_discretize_kernel, out_shape=[shape] * 4, name="ssm_discretize",
    )(rows(a_re), rows(a_im), rows(jnp.broadcast_to(log_dt[:, None], (g, p))), flat(b_re), flat(b_im))

    pick = lambda x: x.reshape(g, gs, p)[:, 0, :].reshape(SUBLANES, gl * p)
    lam_t = jnp.concatenate([pick(l_re), pick(l_im)], axis=1)
    eye = jnp.eye(gl, dtype=F32)

    def bmap(x):
        x = x.reshape(SUBLANES, gl, gs, p)
        return jnp.einsum('jghp,gk->jghkp', x, eye).reshape(SUBLANES, gl * gs, gl * p)

    bmat = jnp.concatenate([bmap(bb_re), bmap(bb_im)], axis=2)

    def cmap(x):
        x = x.astype(F32).reshape(SUBLANES, gl, gs, p)
        return jnp.einsum('cghp,gk->cgpkh', x, eye).reshape(SUBLANES, gl * p, gl * gs)

    cmat = jnp.concatenate([cmap(c_re), -cmap(c_im)], axis=1)
    return lam_t, bmat.astype(BF16), cmat.astype(BF16)


def _ssm_branch(proj, conv_gated, bsz, seq, lam_t, bmat, cmat, d_skip, w_val, w_gate, b_gate1, sl):
    t = proj.shape[0]
    w = d_skip.shape[0]
    d = w_val.shape[1]
    ns = seq // sl
    n_tiles = w // LANES
    assert n_tiles == SUBLANES and d == 2 * w and seq % sl == 0
    u_col = proj.shape[1] // w - 1
    row = lambda a: a.reshape(1, -1)
    const = lambda b, i: (0, 0)
    const3 = lambda b, i: (0, 0, 0)
    return pl.pallas_call(
        _ssm_kernel,
        grid=(bsz, ns),
        in_specs=[
            pl.BlockSpec((sl, w), lambda b, i: (b * ns + i, u_col)),
            pl.BlockSpec((sl, d), lambda b, i: (b * ns + i, 0)),
            pl.BlockSpec((sl, d), lambda b, i: (b * ns + i, 2)),
            pl.BlockSpec((SUBLANES, LANES, w), const3),
            pl.BlockSpec((SUBLANES, w), const),
            pl.BlockSpec((SUBLANES, w, LANES), const3),
            pl.BlockSpec((1, w), const),
            pl.BlockSpec((w, d), const),
            pl.BlockSpec((w, d), const),
            pl.BlockSpec((1, d), const),
        ],
        out_specs=pl.BlockSpec((sl, d), lambda b, i: (b * ns + i, 0)),
        out_shape=jax.ShapeDtypeStruct((t, d), F32),
        scratch_shapes=[
            pltpu.VMEM((n_tiles, sl * SUBLANES, LANES), F32),
            pltpu.VMEM((n_tiles, SUBLANES, LANES), F32),
        ],
        compiler_params=_params(2),
        name="ssm_branch",
    )(proj, conv_gated, proj, bmat, lam_t, cmat, row(d_skip), w_val, w_gate, row(b_gate1))


def _out_proj_kernel(m_ref, x_ref, w_ref, g_ref, x2_ref, h2t_ref):
    x2 = x_ref[...] + jnp.dot(m_ref[...].astype(BF16), w_ref[...], preferred_element_type=F32)
    x2_ref[...] = x2
    h2t_ref[...] = _rms(x2, g_ref[...]).T.astype(BF16)


def _out_proj(merged, x2d, w, g, tm):
    t, d = x2d.shape
    return pl.pallas_call(
        _out_proj_kernel,
        grid=(t // tm,),
        in_specs=[
            pl.BlockSpec((tm, d), lambda i: (i, 0)),
            pl.BlockSpec((tm, d), lambda i: (i, 0)),
            pl.BlockSpec((d, d), lambda i: (0, 0)),
            pl.BlockSpec((1, d), lambda i: (0, 0)),
        ],
        out_specs=[pl.BlockSpec((tm, d), lambda i: (i, 0)), pl.BlockSpec((d, tm), lambda i: (0, i))],
        out_shape=[jax.ShapeDtypeStruct((t, d), F32), jax.ShapeDtypeStruct((d, t), BF16)],
        compiler_params=_params(1),
        name="out_proj",
    )(merged, x2d, w, g.reshape(1, d))


def _sort_pairs(n):
    pairs = []

    def merge(lo, cnt, r):
        step = 2 * r
        if step < cnt:
            merge(lo, cnt, step)
            merge(lo + r, cnt, step)
            pairs.extend((i, i + r) for i in range(lo + r, lo + cnt - r, step))
        else:
            pairs.append((lo, lo + r))

    def sort(lo, cnt):
        if cnt > 1:
            sort(lo, cnt // 2)
            sort(lo + cnt // 2, cnt // 2)
            merge(lo, cnt, 1)

    sort(0, n)
    return pairs


def _bitonic_pairs(n):
    pairs = []
    stride = n // 2
    while stride:
        pairs.extend((i, i + stride) for i in range(n) if not i & stride)
        stride //= 2
    return pairs


def _apply_pairs(v, pairs):
    v = list(v)
    for i, j in pairs:
        if v[j] is None:
            continue
        if v[i] is None:
            v[i], v[j] = v[j], None
        else:
            v[i], v[j] = jnp.maximum(v[i], v[j]), jnp.minimum(v[i], v[j])
    return v


def _top_sorted(v, k):
    v = _apply_pairs(v, _sort_pairs(k))
    shift = SUBLANES // 2
    while shift:
        w = [None if x is None else pltpu.roll(x, shift, axis=0) for x in v]
        merged = []
        for i in range(k):
            x, y = v[i], w[k - 1 - i]
            merged.append(y if x is None else x if y is None else jnp.maximum(x, y))
        v = _apply_pairs(merged, _bitonic_pairs(k))
        shift //= 2
    return v


def _stack_sublanes(rows, sub):
    out = rows[0]
    for j in range(1, len(rows)):
        out = jnp.where(sub == j, rows[j], out)
    return out


def _peer_query_kernel(h2t_ref, wq_ref, sk_ref, a1_ref, b2_ref, th_ref, qt_ref, s_ref):
    tt = h2t_ref.shape[1]
    n_hc, nk, half = sk_ref.shape
    n_heads = n_hc // 2
    k = PEER_TOPK
    assert k == 2 * SUBLANES and nk == k * SUBLANES

    qt_ref[...] = jnp.dot(wq_ref[...], h2t_ref[...], preferred_element_type=F32).astype(BF16)
    for hc in range(n_hc):
        s_ref[hc] = jnp.dot(sk_ref[hc], qt_ref[hc * half:(hc + 1) * half, :],
                            preferred_element_type=F32)

    sub = lax.broadcasted_iota(jnp.int32, (SUBLANES, LANES), 0)

    def sublane_reduce(x, op):
        shift = SUBLANES // 2
        while shift:
            x = op(x, pltpu.roll(x, shift, axis=0))
            shift //= 2
        return x

    def head(h, carry):
        for lt in range(tt // LANES):
            ls = slice(lt * LANES, (lt + 1) * LANES)
            tiles = lambda hc: [s_ref[hc, i * SUBLANES:(i + 1) * SUBLANES, ls] for i in range(k)]
            v1 = _top_sorted(tiles(2 * h), k)
            v2 = _top_sorted(tiles(2 * h + 1), k)
            lo2 = _stack_sublanes(v2[:SUBLANES], sub)
            hi2 = _stack_sublanes(v2[SUBLANES:], sub)
            hi1 = _stack_sublanes(v1[SUBLANES:], sub)
            parts = [(v1[0], lo2), (v1[0], hi2)] + [(v1[r], lo2) for r in range(1, SUBLANES)] \
                + [(hi1, v2[0])]
            cand = [p + q for p, q in parts]
            best = _top_sorted(cand + [None] * (k - len(cand)), k)
            m = best[0]
            tau = best[k - 1]
            picked = [c >= tau for c in cand]
            z = sum(jnp.where(pk, jnp.exp(c - m), 0.0) for pk, c in zip(picked, cand))
            z = sublane_reduce(z, jnp.add)
            shift = m * LOG2E + jnp.log2(z) + 1.0
            dd = [jnp.where(pk, (p * LOG2E - shift) + q * LOG2E, jnp.inf)
                  for pk, (p, q) in zip(picked, parts)]
            th = dd[0]
            for x in dd[1:]:
                th = jnp.minimum(th, x)
            th_ref[h, :, ls] = sublane_reduce(th, jnp.minimum)
            a1_ref[h, :, ls] = s_ref[2 * h, :, ls] * LOG2E - shift[0:1, :]
            b2_ref[h, :, ls] = s_ref[2 * h + 1, :, ls] * LOG2E
        return carry

    lax.fori_loop(0, n_heads, head, 0)


def _peer_query(h2t, wq_t, sk, tt):
    d, t = h2t.shape
    hq = wq_t.shape[0]
    n_hc, nk, half = sk.shape
    n_heads = n_hc // 2
    return pl.pallas_call(
        _peer_query_kernel,
        grid=(t // tt,),
        in_specs=[
            pl.BlockSpec((d, tt), lambda i: (0, i)),
            pl.BlockSpec((hq, d), lambda i: (0, 0)),
            pl.BlockSpec((n_hc, nk, half), lambda i: (0, 0, 0)),
        ],
        out_specs=[
            pl.BlockSpec((n_heads, nk, tt), lambda i: (0, 0, i)),
            pl.BlockSpec((n_heads, nk, tt), lambda i: (0, 0, i)),
            pl.BlockSpec((n_heads, SUBLANES, tt), lambda i: (0, 0, i)),
        ],
        out_shape=[
            jax.ShapeDtypeStruct((n_heads, nk, t), F32),
            jax.ShapeDtypeStruct((n_heads, nk, t), F32),
            jax.ShapeDtypeStruct((n_heads, SUBLANES, t), F32),
        ],
        scratch_shapes=[
            pltpu.VMEM((hq, tt), BF16),
            pltpu.VMEM((n_hc, nk, tt), F32),
        ],
        compiler_params=_params(1),
        name="peer_query",
    )(h2t, wq_t, sk)


GELU_C1 = math.sqrt(2.0 / math.pi)
GELU_C3 = GELU_C1 * 0.044715


def _zero_after(x):
    bits = pltpu.bitcast(x, jnp.uint32)
    bits = lax.shift_right_logical(lax.shift_right_logical(bits, jnp.uint32(16)), jnp.uint32(16))
    return pltpu.bitcast(bits, F32)


def _expert_weights(act_ref, w_ref, a1_ref, b2_ref, th_ref, i1_row0, anchors):
    e_tile, tt = act_ref.shape
    n_heads, nk, _ = b2_ref.shape
    pair = 2 * SUBLANES
    blocks = [(lt, i1) for lt in range(tt // LANES) for i1 in range(e_tile // nk)]
    for (lt, i1), anchor in zip(blocks, anchors):
        ls = slice(lt * LANES, (lt + 1) * LANES)
        zero = _zero_after(anchor)
        th = [th_ref[h, :, ls] + zero for h in range(n_heads)]
        a = [jnp.broadcast_to(a1_ref[h, i1_row0 + i1:i1_row0 + i1 + 1, ls], (SUBLANES, LANES))
             for h in range(n_heads)]
        for blk in range(nk // pair):
            halves = []
            for half in range(2):
                r2 = blk * pair + half * SUBLANES
                g = None
                for h in range(n_heads):
                    dd = a[h] + b2_ref[h, r2:r2 + SUBLANES, ls]
                    term = jnp.where(dd >= th[h], jnp.exp2(dd), 0.0)
                    g = term if g is None else g + term
                x = act_ref[i1 * nk + r2:i1 * nk + r2 + SUBLANES, ls]
                halves.append((g * x) * (1.0 + jnp.tanh(x * (GELU_C1 + GELU_C3 * (x * x)))))
            r = i1 * nk + blk * pair
            w_ref[r:r + pair, ls] = jnp.concatenate(halves, axis=0).astype(BF16)


def _peer_expert_kernel(h2t_ref, a1p_ref, b2p_ref, thp_ref, a1c_ref, b2c_ref, thc_ref, u_ref, vt_ref,
                        x2_ref, gf_ref, o_ref, acc_ref, act0_ref, act1_ref, w0_ref, w1_ref, *, n_e):
    g = pl.program_id(0)
    e_tile, tt = act0_ref.shape
    nk = b2c_ref.shape[1]
    n_i1 = e_tile // nk

    @pl.when(g == 0)
    def _():
        acc_ref[...] = jnp.zeros_like(acc_ref)
        act1_ref[...] = jnp.zeros_like(act1_ref)
        w0_ref[...] = jnp.zeros_like(w0_ref)

    def tick(u_rows, vt_cols, act_new, act_old, w_old, w_new, a1_ref, b2_ref, th_ref, i1_row0):
        half = (tt // LANES) * n_i1 // 2
        new_act = jnp.dot(u_ref[u_rows, :], h2t_ref[...], preferred_element_type=F32)
        upd = jnp.dot(vt_ref[:, vt_cols], w_old[...], preferred_element_type=F32)
        anchors = [new_act[0:SUBLANES, 0:LANES]] * half
        anchors += [new_act[0:SUBLANES, tt // 2:tt // 2 + LANES]] * half
        _expert_weights(act_old, w_new, a1_ref, b2_ref, th_ref, i1_row0, anchors)
        act_new[...] = new_act
        acc_ref[...] += upd

    tick(slice(0, e_tile), slice(0, e_tile), act0_ref, act1_ref, w0_ref, w1_ref,
         a1p_ref, b2p_ref, thp_ref, n_i1)
    tick(slice(e_tile, 2 * e_tile), slice(e_tile, 2 * e_tile), act1_ref, act0_ref, w1_ref, w0_ref,
         a1c_ref, b2c_ref, thc_ref, 0)

    @pl.when(jnp.logical_and(g > 0, (2 * g) % n_e == 0))
    def _():
        x3 = x2_ref[...] + acc_ref[...].T
        o_ref[...] = _rms(x3, gf_ref[...])
        acc_ref[...] = jnp.zeros_like(acc_ref)


def _peer_expert(h2t, a1, b2, th, u, vt, x2, g_final, tt, e_tile):
    d, t = h2t.shape
    n_heads, nk, _ = a1.shape
    n_exp = u.shape[0]
    n_t = t // tt
    n_e = n_exp // e_tile
    assert n_e % 2 == 0 and 2 * e_tile == SUBLANES * nk
    half = n_e // 2
    steps = n_t * half + 1
    once = pl.Buffered(1)

    tok_a = lambda g: jnp.minimum((2 * g) // n_e, n_t - 1)
    tok_p = lambda g: jnp.maximum(2 * g - 1, 0) // n_e
    tok_c = lambda g: jnp.maximum(2 * g - 2, 0) // n_e
    meta = lambda shape, tok: pl.BlockSpec(shape, lambda g: (0, 0, tok(g)), pipeline_mode=once)
    a1_spec = lambda tick, tok: pl.BlockSpec((n_heads, SUBLANES, tt),
                                             lambda g: (0, (tick(g) % n_e) // 2, tok(g)))
    return pl.pallas_call(
        functools.partial(_peer_expert_kernel, n_e=n_e),
        grid=(steps,),
        in_specs=[
            pl.BlockSpec((d, tt), lambda g: (0, tok_a(g)), pipeline_mode=once),
            a1_spec(lambda g: 2 * g - 1, tok_p), meta((n_heads, nk, tt), tok_p), meta((n_heads, SUBLANES, tt), tok_p),
            a1_spec(lambda g: 2 * g, tok_a), meta((n_heads, nk, tt), tok_a), meta((n_heads, SUBLANES, tt), tok_a),
            pl.BlockSpec((2 * e_tile, d), lambda g: (g % half, 0)),
            pl.BlockSpec((d, 2 * e_tile), lambda g: (0, jnp.maximum(g - 1, 0) % half)),
            pl.BlockSpec((tt, d), lambda g: (tok_c(g), 0), pipeline_mode=once),
            pl.BlockSpec((1, d), lambda g: (0, 0)),
        ],
        out_specs=pl.BlockSpec((tt, d), lambda g: (tok_c(g), 0)),
        out_shape=jax.ShapeDtypeStruct((t, d), F32),
        scratch_shapes=[
            pltpu.VMEM((d, tt), F32),
            pltpu.VMEM((e_tile, tt), F32),
            pltpu.VMEM((e_tile, tt), F32),
            pltpu.VMEM((e_tile, tt), BF16),
            pltpu.VMEM((e_tile, tt), BF16),
        ],
        compiler_params=_params(1),
        name="peer_expert",
    )(h2t, a1, b2, th, a1, b2, th, u, vt, x2, g_final.reshape(1, d))


def _cast_transpose_kernel(v_ref, o_ref):
    o_ref[...] = v_ref[...].T.astype(BF16)


def _cast_transpose(v, rows):
    n, d = v.shape
    return pl.pallas_call(
        _cast_transpose_kernel,
        grid=(n // rows,),
        in_specs=[pl.BlockSpec((rows, d), lambda i: (i, 0))],
        out_specs=pl.BlockSpec((d, rows), lambda i: (0, i)),
        out_shape=jax.ShapeDtypeStruct((d, n), BF16),
        compiler_params=_params(1),
        name="cast_transpose",
    )(v)


IN_PROJ_ROWS, IN_PROJ_COLS = 1024, 1792
CONV_SEQ = 256
SSM_SEQ = 256
OUT_PROJ_ROWS = 512
QUERY_TOKENS = 512
EXPERT_TOKENS = 512
TRANSPOSE_ROWS = 1024
EXPERT_I1_ROWS = SUBLANES // 2


def _tile(n, pref):
    t = min(n, pref)
    assert n % t == 0
    return t


def kernel(x, norm_mix, w_in, b_gate, conv_w_dw, conv_b_dw, conv_ln_g, conv_ln_b, conv_w_out, ssm_a_re, ssm_a_im, ssm_log_dt, ssm_b_re, ssm_b_im, ssm_c_re, ssm_c_im, ssm_d, ssm_w_val, ssm_w_gate, w_out, norm_ffn, peer_w_q, peer_sub_keys, peer_u, peer_v, norm_final):
    bsz, seq, d = x.shape
    depth = w_in.shape[0]
    t = bsz * seq
    cw = conv_w_dw.shape[-1]
    sw = ssm_d.shape[-1]
    n_heads, _, nk, half = peer_sub_keys.shape[1:]
    assert 2 * cw == d and 2 * sw == d and peer_u.shape[1] == nk * nk

    assert depth == 1, "the final rmsnorm is fused into the PEER expert stage"
    l = 0
    x2d = x.reshape(t, d).astype(F32)
    w = w_in[l]
    w_perm = jnp.concatenate([w[:, :2 * cw], w[:, 2 * cw + sw:], w[:, 2 * cw:2 * cw + sw]],
                             axis=1).astype(BF16)
    proj = _in_proj(x2d, norm_mix[l], w_perm, _tile(t, IN_PROJ_ROWS), IN_PROJ_COLS)

    conv_gated = _conv_branch(proj, bsz, seq, conv_w_dw[l], conv_b_dw[l], conv_ln_g[l],
                              conv_ln_b[l], conv_w_out[l].astype(BF16), b_gate[l, :d],
                              _tile(seq, CONV_SEQ))

    lam_t, bmat, cmat = _ssm_tables(ssm_a_re[l], ssm_a_im[l], ssm_log_dt[l], ssm_b_re[l],
                                    ssm_b_im[l], ssm_c_re[l], ssm_c_im[l])
    merged = _ssm_branch(proj, conv_gated, bsz, seq, lam_t, bmat, cmat, ssm_d[l],
                         ssm_w_val[l].astype(BF16), ssm_w_gate[l].astype(BF16), b_gate[l, d:],
                         _tile(seq, SSM_SEQ))

    x2, h2t = _out_proj(merged, x2d, w_out[l].astype(BF16), norm_ffn[l], _tile(t, OUT_PROJ_ROWS))

    a1, b2, th = _peer_query(h2t, peer_w_q[l].T.astype(BF16),
                             peer_sub_keys[l].reshape(2 * n_heads, nk, half).astype(BF16),
                             _tile(t, QUERY_TOKENS))
    out = _peer_expert(h2t, a1, b2, th, peer_u[l].astype(BF16), _cast_transpose(peer_v[l], _tile(nk * nk, TRANSPOSE_ROWS)), x2,
                       norm_final, _tile(t, EXPERT_TOKENS), EXPERT_I1_ROWS * nk)
    return out.reshape(bsz, seq, d).astype(x.dtype)
```

```python
import functools
import math

import jax
import jax.numpy as jnp
from jax import lax
from jax.experimental import pallas as pl
from jax.experimental.pallas import tpu as pltpu

F32 = jnp.float32
BF16 = jnp.bfloat16

RMS_EPS = 1e-6
LN_EPS = 1e-5
PEER_TOPK = 16
LOG2E = 1.0 / math.log(2.0)

SUBLANES = 8
LANES = 128
VMEM_LIMIT_BYTES = 58 * 1024 * 1024


def _params(n_axes):
    return pltpu.CompilerParams(
        dimension_semantics=("arbitrary",) * n_axes,
        vmem_limit_bytes=VMEM_LIMIT_BYTES,
    )


def _rms(x, g):
    ms = jnp.mean(x * x, axis=-1, keepdims=True)
    return x * lax.rsqrt(ms + RMS_EPS) * g


def _in_proj_kernel(x_ref, g_ref, w_ref, o_ref, h_ref):
    @pl.when(pl.program_id(1) == 0)
    def _():
        h_ref[...] = _rms(x_ref[...], g_ref[...]).astype(BF16)

    o_ref[...] = jnp.dot(h_ref[...], w_ref[...], preferred_element_type=F32)


def _in_proj(x2d, g, w, tm, tn):
    t, d = x2d.shape
    n = w.shape[1]
    return pl.pallas_call(
        _in_proj_kernel,
        grid=(t // tm, n // tn),
        in_specs=[
            pl.BlockSpec((tm, d), lambda i, j: (i, 0)),
            pl.BlockSpec((1, d), lambda i, j: (0, 0)),
            pl.BlockSpec((d, tn), lambda i, j: (0, j)),
        ],
        out_specs=pl.BlockSpec((tm, tn), lambda i, j: (i, j)),
        out_shape=jax.ShapeDtypeStruct((t, n), F32),
        scratch_shapes=[pltpu.VMEM((tm, d), BF16)],
        compiler_params=_params(2),
        name="in_proj",
    )(x2d, g.reshape(1, d), w)


CONV_HALO = 32
CONV_ROWS = 32
CONV_COLS = 512


def _conv_kernel(a_ref, b_ref, ah_ref, bh_ref, gate_ref, wdw_ref, bdw_ref, lng_ref, lnb_ref,
                 wpw_ref, bg_ref, o_ref, buf_ref, sh_ref, y_ref):
    ts, c = a_ref.shape
    k_taps = wdw_ref.shape[0]
    n = ts + CONV_HALO
    first = pl.program_id(1) == 0

    glu = a_ref[...] * jax.nn.sigmoid(b_ref[...])
    halo = ah_ref[...] * jax.nn.sigmoid(bh_ref[...])
    buf_ref[0:CONV_HALO, :] = jnp.where(first, 0.0, halo)
    buf_ref[CONV_HALO:n, :] = glu
    buf_ref[n:n + SUBLANES, :] = jnp.zeros((SUBLANES, c), F32)
    for r in range(1, SUBLANES):
        sh_ref[r - 1] = buf_ref[r:r + n, :]

    off = CONV_HALO - (k_taps - 1)

    def rows(rc, carry):
        r0 = pl.multiple_of(rc * CONV_ROWS, CONV_ROWS)
        for lc in range(c // CONV_COLS):
            cols = slice(lc * CONV_COLS, (lc + 1) * CONV_COLS)
            acc = [None] * (CONV_ROWS // SUBLANES)
            for k in range(k_taps):
                q, r = divmod(k + off, SUBLANES)
                wk = wdw_ref[k, :, cols]
                for rt in range(len(acc)):
                    rows_k = pl.ds(r0 + SUBLANES * (q + rt), SUBLANES)
                    win = buf_ref[rows_k, cols] if r == 0 else sh_ref[r - 1, rows_k, cols]
                    acc[rt] = win * wk if acc[rt] is None else acc[rt] + win * wk
            for rt in range(len(acc)):
                y_ref[pl.ds(r0 + SUBLANES * rt, SUBLANES), cols] = acc[rt] + bdw_ref[:, cols]
        return carry

    lax.fori_loop(0, ts // CONV_ROWS, rows, 0)

    y = y_ref[...]
    mu = jnp.mean(y, axis=-1, keepdims=True)
    dlt = y - mu
    var = jnp.mean(dlt * dlt, axis=-1, keepdims=True)
    yn = dlt * lax.rsqrt(var + LN_EPS) * lng_ref[...] + lnb_ref[...]
    z = (yn * jax.nn.sigmoid(yn)).astype(BF16)
    out = jnp.dot(z, wpw_ref[...], preferred_element_type=F32)
    o_ref[...] = jax.nn.sigmoid(gate_ref[...] + bg_ref[...]) * out


def _conv_branch(proj, bsz, seq, w_dw, b_dw, ln_g, ln_b, w_pw, b_gate0, ts):
    t = proj.shape[0]
    k_taps, c = w_dw.shape
    d = w_pw.shape[1]
    ns = seq // ts
    hb = ts // CONV_HALO
    assert d == 2 * c and k_taps - 1 <= CONV_HALO and ts % CONV_ROWS == 0 and seq % ts == 0

    def halo_idx(col):
        return lambda b, i: (jnp.maximum((b * ns + i) * hb - 1, 0), col)

    row = lambda a: a.reshape(1, -1)
    return pl.pallas_call(
        _conv_kernel,
        grid=(bsz, ns),
        in_specs=[
            pl.BlockSpec((ts, c), lambda b, i: (b * ns + i, 0)),
            pl.BlockSpec((ts, c), lambda b, i: (b * ns + i, 1)),
            pl.BlockSpec((CONV_HALO, c), halo_idx(0)),
            pl.BlockSpec((CONV_HALO, c), halo_idx(1)),
            pl.BlockSpec((ts, d), lambda b, i: (b * ns + i, 1)),
            pl.BlockSpec((k_taps, SUBLANES, c), lambda b, i: (0, 0, 0)),
            pl.BlockSpec((1, c), lambda b, i: (0, 0)),
            pl.BlockSpec((1, c), lambda b, i: (0, 0)),
            pl.BlockSpec((1, c), lambda b, i: (0, 0)),
            pl.BlockSpec((c, d), lambda b, i: (0, 0)),
            pl.BlockSpec((1, d), lambda b, i: (0, 0)),
        ],
        out_specs=pl.BlockSpec((ts, d), lambda b, i: (b * ns + i, 0)),
        out_shape=jax.ShapeDtypeStruct((t, d), F32),
        scratch_shapes=[
            pltpu.VMEM((ts + CONV_HALO + SUBLANES, c), F32),
            pltpu.VMEM((SUBLANES - 1, ts + CONV_HALO, c), F32),
            pltpu.VMEM((ts, c), F32),
        ],
        compiler_params=_params(2),
        name="conv_branch",
    )(proj, proj, proj, proj, proj, jnp.broadcast_to(w_dw[:, None, :], (k_taps, SUBLANES, c)),
      row(b_dw), row(ln_g), row(ln_b), w_pw, row(b_gate0))


def _ssm_kernel(u_ref, cg_ref, gate_ref, bmat_ref, lam_ref, cmat_ref, dskip_ref, wval_ref,
                wgate_ref, bg_ref, o_ref, st_ref, carry_ref):
    sl, w = u_ref.shape
    n_tiles = w // LANES
    hw = n_tiles // 2

    @pl.when(pl.program_id(1) == 0)
    def _():
        carry_ref[...] = jnp.zeros_like(carry_ref)

    for j in range(SUBLANES):
        bu = jnp.dot(u_ref[:, j * LANES:(j + 1) * LANES].astype(BF16), bmat_ref[j],
                     preferred_element_type=F32)
        for lt in range(n_tiles):
            st_ref[lt, pl.ds(j, sl, stride=SUBLANES), :] = bu[:, lt * LANES:(lt + 1) * LANES]

    lam = [lam_ref[:, lt * LANES:(lt + 1) * LANES] for lt in range(n_tiles)]

    def step(t, s):
        r = pl.multiple_of(t * SUBLANES, SUBLANES)
        new = [None] * n_tiles
        for q in range(hw):
            sr, si, ar, ai = s[q], s[q + hw], lam[q], lam[q + hw]
            new[q] = ar * sr - ai * si + st_ref[q, pl.ds(r, SUBLANES), :]
            new[q + hw] = ar * si + ai * sr + st_ref[q + hw, pl.ds(r, SUBLANES), :]
        for lt in range(n_tiles):
            st_ref[lt, pl.ds(r, SUBLANES), :] = new[lt]
        return tuple(new)

    s_end = lax.fori_loop(0, sl, step, tuple(carry_ref[lt] for lt in range(n_tiles)), unroll=4)
    for lt in range(n_tiles):
        carry_ref[lt] = s_end[lt]

    ys = []
    for j in range(SUBLANES):
        sj = jnp.concatenate([st_ref[lt, pl.ds(j, sl, stride=SUBLANES), :] for lt in range(n_tiles)],
                             axis=1)
        ys.append(jnp.dot(sj.astype(BF16), cmat_ref[j], preferred_element_type=F32))
    y = jnp.concatenate(ys, axis=1)

    y = y + dskip_ref[...] * u_ref[...]
    z = jax.nn.gelu(y).astype(BF16)
    val = jnp.dot(z, wval_ref[...], preferred_element_type=F32)
    gt = jnp.dot(z, wgate_ref[...], preferred_element_type=F32)
    branch = val * jax.nn.sigmoid(gt)
    o_ref[...] = cg_ref[...] + jax.nn.sigmoid(gate_ref[...] + bg_ref[...]) * branch


def _ssm_discretize_kernel(are_ref, aim_ref, ldt_ref, bre_ref, bim_ref, lre_ref, lim_ref, obr_ref, obi_ref):
    a_re = are_ref[...]
    a_im = aim_ref[...]
    dt = jnp.exp(ldt_ref[...])
    mag = jnp.exp(a_re * dt)
    l_re = mag * jnp.cos(a_im * dt)
    l_im = mag * jnp.sin(a_im * dt)
    den = a_re * a_re + a_im * a_im
    c_re = ((l_re - 1.0) * a_re + l_im * a_im) / den
    c_im = (l_im * a_re - (l_re - 1.0) * a_im) / den
    lre_ref[...] = l_re
    lim_ref[...] = l_im
    obr_ref[...] = c_re * bre_ref[...] - c_im * bim_ref[...]
    obi_ref[...] = c_re * bim_ref[...] + c_im * bre_ref[...]


def _ssm_tables(a_re, a_im, log_dt, b_re, b_im, c_re, c_im):
    g, p = a_re.shape
    gs = b_re.shape[-1]
    assert g % SUBLANES == 0
    gl = g // SUBLANES
    rows = lambda x: jnp.repeat(x.astype(F32), gs, axis=0)
    flat = lambda x: jnp.swapaxes(x.astype(F32), 1, 2).reshape(g * gs, p)
    shape = jax.ShapeDtypeStruct((g * gs, p), F32)
    l_re, l_im, bb_re, bb_im = pl.pallas_call(
        _ssm_discretize_kernel, out_shape=[shape] * 4, name="ssm_discretize",
    )(rows(a_re), rows(a_im), rows(jnp.broadcast_to(log_dt[:, None], (g, p))), flat(b_re), flat(b_im))

    pick = lambda x: x.reshape(g, gs, p)[:, 0, :].reshape(SUBLANES, gl * p)
    lam_t = jnp.concatenate([pick(l_re), pick(l_im)], axis=1)
    eye = jnp.eye(gl, dtype=F32)

    def bmap(x):
        x = x.reshape(SUBLANES, gl, gs, p)
        return jnp.einsum('jghp,gk->jghkp', x, eye).reshape(SUBLANES, gl * gs, gl * p)

    bmat = jnp.concatenate([bmap(bb_re), bmap(bb_im)], axis=2)

    def cmap(x):
        x = x.astype(F32).reshape(SUBLANES, gl, gs, p)
        return jnp.einsum('cghp,gk->cgpkh', x, eye).reshape(SUBLANES, gl * p, gl * gs)

    cmat = jnp.concatenate([cmap(c_re), -cmap(c_im)], axis=1)
    return lam_t, bmat.astype(BF16), cmat.astype(BF16)


def _ssm_branch(proj, conv_gated, bsz, seq, lam_t, bmat, cmat, d_skip, w_val, w_gate, b_gate1, sl):
    t = proj.shape[0]
    w = d_skip.shape[0]
    d = w_val.shape[1]
    ns = seq // sl
    n_tiles = w // LANES
    assert n_tiles == SUBLANES and d == 2 * w and seq % sl == 0
    u_col = proj.shape[1] // w - 1
    row = lambda a: a.reshape(1, -1)
    const = lambda b, i: (0, 0)
    const3 = lambda b, i: (0, 0, 0)
    return pl.pallas_call(
        _ssm_kernel,
        grid=(bsz, ns),
        in_specs=[
            pl.BlockSpec((sl, w), lambda b, i: (b * ns + i, u_col)),
            pl.BlockSpec((sl, d), lambda b, i: (b * ns + i, 0)),
            pl.BlockSpec((sl, d), lambda b, i: (b * ns + i, 2)),
            pl.BlockSpec((SUBLANES, LANES, w), const3),
            pl.BlockSpec((SUBLANES, w), const),
            pl.BlockSpec((SUBLANES, w, LANES), const3),
            pl.BlockSpec((1, w), const),
            pl.BlockSpec((w, d), const),
            pl.BlockSpec((w, d), const),
            pl.BlockSpec((1, d), const),
        ],
        out_specs=pl.BlockSpec((sl, d), lambda b, i: (b * ns + i, 0)),
        out_shape=jax.ShapeDtypeStruct((t, d), F32),
        scratch_shapes=[
            pltpu.VMEM((n_tiles, sl * SUBLANES, LANES), F32),
            pltpu.VMEM((n_tiles, SUBLANES, LANES), F32),
        ],
        compiler_params=_params(2),
        name="ssm_branch",
    )(proj, conv_gated, proj, bmat, lam_t, cmat, row(d_skip), w_val, w_gate, row(b_gate1))


def _out_proj_kernel(m_ref, x_ref, w_ref, g_ref, x2_ref, h2t_ref):
    x2 = x_ref[...] + jnp.dot(m_ref[...].astype(BF16), w_ref[...], preferred_element_type=F32)
    x2_ref[...] = x2
    h2t_ref[...] = _rms(x2, g_ref[...]).T.astype(BF16)


def _out_proj(merged, x2d, w, g, tm):
    t, d = x2d.shape
    return pl.pallas_call(
        _out_proj_kernel,
        grid=(t // tm,),
        in_specs=[
            pl.BlockSpec((tm, d), lambda i: (i, 0)),
            pl.BlockSpec((tm, d), lambda i: (i, 0)),
            pl.BlockSpec((d, d), lambda i: (0, 0)),
            pl.BlockSpec((1, d), lambda i: (0, 0)),
        ],
        out_specs=[pl.BlockSpec((tm, d), lambda i: (i, 0)), pl.BlockSpec((d, tm), lambda i: (0, i))],
        out_shape=[jax.ShapeDtypeStruct((t, d), F32), jax.ShapeDtypeStruct((d, t), BF16)],
        compiler_params=_params(1),
        name="out_proj",
    )(merged, x2d, w, g.reshape(1, d))


def _sort_pairs(n):
    pairs = []

    def merge(lo, cnt, r):
        step = 2 * r
        if step < cnt:
            merge(lo, cnt, step)
            merge(lo + r, cnt, step)
            pairs.extend((i, i + r) for i in range(lo + r, lo + cnt - r, step))
        else:
            pairs.append((lo, lo + r))

    def sort(lo, cnt):
        if cnt > 1:
            sort(lo, cnt // 2)
            sort(lo + cnt // 2, cnt // 2)
            merge(lo, cnt, 1)

    sort(0, n)
    return pairs


def _bitonic_pairs(n):
    pairs = []
    stride = n // 2
    while stride:
        pairs.extend((i, i + stride) for i in range(n) if not i & stride)
        stride //= 2
    return pairs


def _apply_pairs(v, pairs):
    v = list(v)
    for i, j in pairs:
        if v[j] is None:
            continue
        if v[i] is None:
            v[i], v[j] = v[j], None
        else:
            v[i], v[j] = jnp.maximum(v[i], v[j]), jnp.minimum(v[i], v[j])
    return v


def _top_sorted(v, k):
    v = _apply_pairs(v, _sort_pairs(k))
    shift = SUBLANES // 2
    while shift:
        w = [None if x is None else pltpu.roll(x, shift, axis=0) for x in v]
        merged = []
        for i in range(k):
            x, y = v[i], w[k - 1 - i]
            merged.append(y if x is None else x if y is None else jnp.maximum(x, y))
        v = _apply_pairs(merged, _bitonic_pairs(k))
        shift //= 2
    return v


def _stack_sublanes(rows, sub):
    out = rows[0]
    for j in range(1, len(rows)):
        out = jnp.where(sub == j, rows[j], out)
    return out


def _peer_query_kernel(h2t_ref, wq_ref, sk_ref, a1_ref, b2_ref, th_ref, qt_ref, s_ref):
    tt = h2t_ref.shape[1]
    n_hc, nk, half = sk_ref.shape
    n_heads = n_hc // 2
    k = PEER_TOPK
    assert k == 2 * SUBLANES and nk == k * SUBLANES

    qt_ref[...] = jnp.dot(wq_ref[...], h2t_ref[...], preferred_element_type=F32).astype(BF16)
    for hc in range(n_hc):
        s_ref[hc] = jnp.dot(sk_ref[hc], qt_ref[hc * half:(hc + 1) * half, :],
                            preferred_element_type=F32)

    sub = lax.broadcasted_iota(jnp.int32, (SUBLANES, LANES), 0)

    def sublane_reduce(x, op):
        shift = SUBLANES // 2
        while shift:
            x = op(x, pltpu.roll(x, shift, axis=0))
            shift //= 2
        return x

    def head(h, carry):
        for lt in range(tt // LANES):
            ls = slice(lt * LANES, (lt + 1) * LANES)
            tiles = lambda hc: [s_ref[hc, i * SUBLANES:(i + 1) * SUBLANES, ls] for i in range(k)]
            v1 = _top_sorted(tiles(2 * h), k)
            v2 = _top_sorted(tiles(2 * h + 1), k)
            lo2 = _stack_sublanes(v2[:SUBLANES], sub)
            hi2 = _stack_sublanes(v2[SUBLANES:], sub)
            hi1 = _stack_sublanes(v1[SUBLANES:], sub)
            parts = [(v1[0], lo2), (v1[0], hi2)] + [(v1[r], lo2) for r in range(1, SUBLANES)] \
                + [(hi1, v2[0])]
            cand = [p + q for p, q in parts]
            best = _top_sorted(cand + [None] * (k - len(cand)), k)
            m = best[0]
            tau = best[k - 1]
            picked = [c >= tau for c in cand]
            z = sum(jnp.where(pk, jnp.exp(c - m), 0.0) for pk, c in zip(picked, cand))
            z = sublane_reduce(z, jnp.add)
            shift = m * LOG2E + jnp.log2(z) + 1.0
            dd = [jnp.where(pk, (p * LOG2E - shift) + q * LOG2E, jnp.inf)
                  for pk, (p, q) in zip(picked, parts)]
            th = dd[0]
            for x in dd[1:]:
                th = jnp.minimum(th, x)
            th_ref[h, :, ls] = sublane_reduce(th, jnp.minimum)
            a1_ref[h, :, ls] = s_ref[2 * h, :, ls] * LOG2E - shift[0:1, :]
            b2_ref[h, :, ls] = s_ref[2 * h + 1, :, ls] * LOG2E
        return carry

    lax.fori_loop(0, n_heads, head, 0)


def _peer_query(h2t, wq_t, sk, tt):
    d, t = h2t.shape
    hq = wq_t.shape[0]
    n_hc, nk, half = sk.shape
    n_heads = n_hc // 2
    return pl.pallas_call(
        _peer_query_kernel,
        grid=(t // tt,),
        in_specs=[
            pl.BlockSpec((d, tt), lambda i: (0, i)),
            pl.BlockSpec((hq, d), lambda i: (0, 0)),
            pl.BlockSpec((n_hc, nk, half), lambda i: (0, 0, 0)),
        ],
        out_specs=[
            pl.BlockSpec((n_heads, nk, tt), lambda i: (0, 0, i)),
            pl.BlockSpec((n_heads, nk, tt), lambda i: (0, 0, i)),
            pl.BlockSpec((n_heads, SUBLANES, tt), lambda i: (0, 0, i)),
        ],
        out_shape=[
            jax.ShapeDtypeStruct((n_heads, nk, t), F32),
            jax.ShapeDtypeStruct((n_heads, nk, t), F32),
            jax.ShapeDtypeStruct((n_heads, SUBLANES, t), F32),
        ],
        scratch_shapes=[
            pltpu.VMEM((hq, tt), BF16),
            pltpu.VMEM((n_hc, nk, tt), F32),
        ],
        compiler_params=_params(1),
        name="peer_query",
    )(h2t, wq_t, sk)


GELU_C1 = math.sqrt(2.0 / math.pi)
GELU_C3 = GELU_C1 * 0.044715


def _zero_after(x):
    bits = pltpu.bitcast(x, jnp.uint32)
    bits = lax.shift_right_logical(lax.shift_right_logical(bits, jnp.uint32(16)), jnp.uint32(16))
    return pltpu.bitcast(bits, F32)


def _expert_weights(act_ref, w_ref, a1_ref, b2_ref, th_ref, i1_row0, anchors):
    e_tile, tt = act_ref.shape
    n_heads, nk, _ = b2_ref.shape
    pair = 2 * SUBLANES
    blocks = [(lt, i1) for lt in range(tt // LANES) for i1 in range(e_tile // nk)]
    for (lt, i1), anchor in zip(blocks, anchors):
        ls = slice(lt * LANES, (lt + 1) * LANES)
        zero = _zero_after(anchor)
        th = [th_ref[h, :, ls] + zero for h in range(n_heads)]
        a = [jnp.broadcast_to(a1_ref[h, i1_row0 + i1:i1_row0 + i1 + 1, ls], (SUBLANES, LANES))
             for h in range(n_heads)]
        for blk in range(nk // pair):
            halves = []
            for half in range(2):
                r2 = blk * pair + half * SUBLANES
                g = None
                for h in range(n_heads):
                    dd = a[h] + b2_ref[h, r2:r2 + SUBLANES, ls]
                    term = jnp.where(dd >= th[h], jnp.exp2(dd), 0.0)
                    g = term if g is None else g + term
                x = act_ref[i1 * nk + r2:i1 * nk + r2 + SUBLANES, ls]
                halves.append((g * x) * (1.0 + jnp.tanh(x * (GELU_C1 + GELU_C3 * (x * x)))))
            r = i1 * nk + blk * pair
            w_ref[r:r + pair, ls] = jnp.concatenate(halves, axis=0).astype(BF16)


def _peer_expert_kernel(h2t_ref, a1p_ref, b2p_ref, thp_ref, a1c_ref, b2c_ref, thc_ref, u_ref, vt_ref,
                        x2_ref, gf_ref, o_ref, acc_ref, act0_ref, act1_ref, w0_ref, w1_ref, *, n_e):
    g = pl.program_id(0)
    e_tile, tt = act0_ref.shape
    nk = b2c_ref.shape[1]
    n_i1 = e_tile // nk

    @pl.when(g == 0)
    def _():
        acc_ref[...] = jnp.zeros_like(acc_ref)
        act1_ref[...] = jnp.zeros_like(act1_ref)
        w0_ref[...] = jnp.zeros_like(w0_ref)

    def tick(u_rows, vt_cols, act_new, act_old, w_old, w_new, a1_ref, b2_ref, th_ref, i1_row0):
        half = (tt // LANES) * n_i1 // 2
        new_act = jnp.dot(u_ref[u_rows, :], h2t_ref[...], preferred_element_type=F32)
        upd = jnp.dot(vt_ref[:, vt_cols], w_old[...], preferred_element_type=F32)
        anchors = [new_act[0:SUBLANES, 0:LANES]] * half
        anchors += [new_act[0:SUBLANES, tt // 2:tt // 2 + LANES]] * half
        _expert_weights(act_old, w_new, a1_ref, b2_ref, th_ref, i1_row0, anchors)
        act_new[...] = new_act
        acc_ref[...] += upd

    tick(slice(0, e_tile), slice(0, e_tile), act0_ref, act1_ref, w0_ref, w1_ref,
         a1p_ref, b2p_ref, thp_ref, n_i1)
    tick(slice(e_tile, 2 * e_tile), slice(e_tile, 2 * e_tile), act1_ref, act0_ref, w1_ref, w0_ref,
         a1c_ref, b2c_ref, thc_ref, 0)

    @pl.when(jnp.logical_and(g > 0, (2 * g) % n_e == 0))
    def _():
        x3 = x2_ref[...] + acc_ref[...].T
        o_ref[...] = _rms(x3, gf_ref[...])
        acc_ref[...] = jnp.zeros_like(acc_ref)


def _peer_expert(h2t, a1, b2, th, u, vt, x2, g_final, tt, e_tile):
    d, t = h2t.shape
    n_heads, nk, _ = a1.shape
    n_exp = u.shape[0]
    n_t = t // tt
    n_e = n_exp // e_tile
    assert n_e % 2 == 0 and 2 * e_tile == SUBLANES * nk
    half = n_e // 2
    steps = n_t * half + 1
    once = pl.Buffered(1)

    tok_a = lambda g: jnp.minimum((2 * g) // n_e, n_t - 1)
    tok_p = lambda g: jnp.maximum(2 * g - 1, 0) // n_e
    tok_c = lambda g: jnp.maximum(2 * g - 2, 0) // n_e
    meta = lambda shape, tok, mode=None: pl.BlockSpec(shape, lambda g: (0, 0, tok(g)), pipeline_mode=mode)
    a1_spec = lambda tick, tok: pl.BlockSpec((n_heads, SUBLANES, tt),
                                             lambda g: (0, (tick(g) % n_e) // 2, tok(g)))
    return pl.pallas_call(
        functools.partial(_peer_expert_kernel, n_e=n_e),
        grid=(steps,),
        in_specs=[
            pl.BlockSpec((d, tt), lambda g: (0, tok_a(g))),
            a1_spec(lambda g: 2 * g - 1, tok_p), meta((n_heads, nk, tt), tok_p, once),
            meta((n_heads, SUBLANES, tt), tok_p, once),
            a1_spec(lambda g: 2 * g, tok_a), meta((n_heads, nk, tt), tok_a), meta((n_heads, SUBLANES, tt), tok_a),
            pl.BlockSpec((2 * e_tile, d), lambda g: (g % half, 0)),
            pl.BlockSpec((d, 2 * e_tile), lambda g: (0, jnp.maximum(g - 1, 0) % half)),
            pl.BlockSpec((tt, d), lambda g: (tok_c(g), 0), pipeline_mode=once),
            pl.BlockSpec((1, d), lambda g: (0, 0)),
        ],
        out_specs=pl.BlockSpec((tt, d), lambda g: (tok_c(g), 0)),
        out_shape=jax.ShapeDtypeStruct((t, d), F32),
        scratch_shapes=[
            pltpu.VMEM((d, tt), F32),
            pltpu.VMEM((e_tile, tt), F32),
            pltpu.VMEM((e_tile, tt), F32),
            pltpu.VMEM((e_tile, tt), BF16),
            pltpu.VMEM((e_tile, tt), BF16),
        ],
        compiler_params=_params(1),
        name="peer_expert",
    )(h2t, a1, b2, th, a1, b2, th, u, vt, x2, g_final.reshape(1, d))


def _cast_transpose_kernel(v_ref, o_ref):
    o_ref[...] = v_ref[...].T.astype(BF16)


def _cast_transpose(v, rows):
    n, d = v.shape
    return pl.pallas_call(
        _cast_transpose_kernel,
        grid=(n // rows,),
        in_specs=[pl.BlockSpec((rows, d), lambda i: (i, 0))],
        out_specs=pl.BlockSpec((d, rows), lambda i: (0, i)),
        out_shape=jax.ShapeDtypeStruct((d, n), BF16),
        compiler_params=_params(1),
        name="cast_transpose",
    )(v)


IN_PROJ_ROWS, IN_PROJ_COLS = 1024, 1792
CONV_SEQ = 256
SSM_SEQ = 256
OUT_PROJ_ROWS = 512
QUERY_TOKENS = 512
EXPERT_TOKENS = 512
TRANSPOSE_ROWS = 1024
EXPERT_I1_ROWS = SUBLANES // 2


def _tile(n, pref):
    t = min(n, pref)
    assert n % t == 0
    return t


def kernel(x, norm_mix, w_in, b_gate, conv_w_dw, conv_b_dw, conv_ln_g, conv_ln_b, conv_w_out, ssm_a_re, ssm_a_im, ssm_log_dt, ssm_b_re, ssm_b_im, ssm_c_re, ssm_c_im, ssm_d, ssm_w_val, ssm_w_gate, w_out, norm_ffn, peer_w_q, peer_sub_keys, peer_u, peer_v, norm_final):
    bsz, seq, d = x.shape
    depth = w_in.shape[0]
    t = bsz * seq
    cw = conv_w_dw.shape[-1]
    sw = ssm_d.shape[-1]
    n_heads, _, nk, half = peer_sub_keys.shape[1:]
    assert 2 * cw == d and 2 * sw == d and peer_u.shape[1] == nk * nk

    assert depth == 1, "the final rmsnorm is fused into the PEER expert stage"
    l = 0
    x2d = x.reshape(t, d).astype(F32)
    w = w_in[l]
    w_perm = jnp.concatenate([w[:, :2 * cw], w[:, 2 * cw + sw:], w[:, 2 * cw:2 * cw + sw]],
                             axis=1).astype(BF16)
    proj = _in_proj(x2d, norm_mix[l], w_perm, _tile(t, IN_PROJ_ROWS), IN_PROJ_COLS)

    conv_gated = _conv_branch(proj, bsz, seq, conv_w_dw[l], conv_b_dw[l], conv_ln_g[l],
                              conv_ln_b[l], conv_w_out[l].astype(BF16), b_gate[l, :d],
                              _tile(seq, CONV_SEQ))

    lam_t, bmat, cmat = _ssm_tables(ssm_a_re[l], ssm_a_im[l], ssm_log_dt[l], ssm_b_re[l],
                                    ssm_b_im[l], ssm_c_re[l], ssm_c_im[l])
    merged = _ssm_branch(proj, conv_gated, bsz, seq, lam_t, bmat, cmat, ssm_d[l],
                         ssm_w_val[l].astype(BF16), ssm_w_gate[l].astype(BF16), b_gate[l, d:],
                         _tile(seq, SSM_SEQ))

    x2, h2t = _out_proj(merged, x2d, w_out[l].astype(BF16), norm_ffn[l], _tile(t, OUT_PROJ_ROWS))

    a1, b2, th = _peer_query(h2t, peer_w_q[l].T.astype(BF16),
                             peer_sub_keys[l].reshape(2 * n_heads, nk, half).astype(BF16),
                             _tile(t, QUERY_TOKENS))
    out = _peer_expert(h2t, a1, b2, th, peer_u[l].astype(BF16), _cast_transpose(peer_v[l], _tile(nk * nk, TRANSPOSE_ROWS)), x2,
                       norm_final, _tile(t, EXPERT_TOKENS), EXPERT_I1_ROWS * nk)
    return out.reshape(bsz, seq, d).astype(x.dtype)
```

```python
import functools
import math

import jax
import jax.numpy as jnp
from jax import lax
from jax.experimental import pallas as pl
from jax.experimental.pallas import tpu as pltpu

F32 = jnp.float32
BF16 = jnp.bfloat16

RMS_EPS = 1e-6
LN_EPS = 1e-5
PEER_TOPK = 16
LOG2E = 1.0 / math.log(2.0)

SUBLANES = 8
LANES = 128
VMEM_LIMIT_BYTES = 60 * 1024 * 1024


def _params(n_axes):
    return pltpu.CompilerParams(
        dimension_semantics=("arbitrary",) * n_axes,
        vmem_limit_bytes=VMEM_LIMIT_BYTES,
    )


def _rms(x, g):
    ms = jnp.mean(x * x, axis=-1, keepdims=True)
    return x * lax.rsqrt(ms + RMS_EPS) * g


def _in_proj_kernel(x_ref, g_ref, w_ref, o_ref, h_ref):
    @pl.when(pl.program_id(1) == 0)
    def _():
        h_ref[...] = _rms(x_ref[...], g_ref[...]).astype(BF16)

    o_ref[...] = jnp.dot(h_ref[...], w_ref[...], preferred_element_type=F32)


def _in_proj(x2d, g, w, tm, tn):
    t, d = x2d.shape
    n = w.shape[1]
    return pl.pallas_call(
        _in_proj_kernel,
        grid=(t // tm, n // tn),
        in_specs=[
            pl.BlockSpec((tm, d), lambda i, j: (i, 0)),
            pl.BlockSpec((1, d), lambda i, j: (0, 0)),
            pl.BlockSpec((d, tn), lambda i, j: (0, j)),
        ],
        out_specs=pl.BlockSpec((tm, tn), lambda i, j: (i, j)),
        out_shape=jax.ShapeDtypeStruct((t, n), F32),
        scratch_shapes=[pltpu.VMEM((tm, d), BF16)],
        compiler_params=_params(2),
        name="in_proj",
    )(x2d, g.reshape(1, d), w)


CONV_HALO = 32
CONV_ROWS = 32
CONV_COLS = 512


def _conv_kernel(a_ref, b_ref, ah_ref, bh_ref, gate_ref, wdw_ref, bdw_ref, lng_ref, lnb_ref,
                 wpw_ref, bg_ref, o_ref, buf_ref, sh_ref, y_ref):
    ts, c = a_ref.shape
    k_taps = wdw_ref.shape[0]
    n = ts + CONV_HALO
    first = pl.program_id(1) == 0

    glu = a_ref[...] * jax.nn.sigmoid(b_ref[...])
    halo = ah_ref[...] * jax.nn.sigmoid(bh_ref[...])
    buf_ref[0:CONV_HALO, :] = jnp.where(first, 0.0, halo)
    buf_ref[CONV_HALO:n, :] = glu
    buf_ref[n:n + SUBLANES, :] = jnp.zeros((SUBLANES, c), F32)
    for r in range(1, SUBLANES):
        sh_ref[r - 1] = buf_ref[r:r + n, :]

    off = CONV_HALO - (k_taps - 1)

    def rows(rc, carry):
        r0 = pl.multiple_of(rc * CONV_ROWS, CONV_ROWS)
        for lc in range(c // CONV_COLS):
            cols = slice(lc * CONV_COLS, (lc + 1) * CONV_COLS)
            acc = [None] * (CONV_ROWS // SUBLANES)
            for k in range(k_taps):
                q, r = divmod(k + off, SUBLANES)
                wk = wdw_ref[k, :, cols]
                for rt in range(len(acc)):
                    rows_k = pl.ds(r0 + SUBLANES * (q + rt), SUBLANES)
                    win = buf_ref[rows_k, cols] if r == 0 else sh_ref[r - 1, rows_k, cols]
                    acc[rt] = win * wk if acc[rt] is None else acc[rt] + win * wk
            for rt in range(len(acc)):
                y_ref[pl.ds(r0 + SUBLANES * rt, SUBLANES), cols] = acc[rt] + bdw_ref[:, cols]
        return carry

    lax.fori_loop(0, ts // CONV_ROWS, rows, 0)

    y = y_ref[...]
    mu = jnp.mean(y, axis=-1, keepdims=True)
    dlt = y - mu
    var = jnp.mean(dlt * dlt, axis=-1, keepdims=True)
    yn = dlt * lax.rsqrt(var + LN_EPS) * lng_ref[...] + lnb_ref[...]
    z = (yn * jax.nn.sigmoid(yn)).astype(BF16)
    out = jnp.dot(z, wpw_ref[...], preferred_element_type=F32)
    o_ref[...] = jax.nn.sigmoid(gate_ref[...] + bg_ref[...]) * out


def _conv_branch(proj, bsz, seq, w_dw, b_dw, ln_g, ln_b, w_pw, b_gate0, ts):
    t = proj.shape[0]
    k_taps, c = w_dw.shape
    d = w_pw.shape[1]
    ns = seq // ts
    hb = ts // CONV_HALO
    assert d == 2 * c and k_taps - 1 <= CONV_HALO and ts % CONV_ROWS == 0 and seq % ts == 0

    def halo_idx(col):
        return lambda b, i: (jnp.maximum((b * ns + i) * hb - 1, 0), col)

    row = lambda a: a.reshape(1, -1)
    return pl.pallas_call(
        _conv_kernel,
        grid=(bsz, ns),
        in_specs=[
            pl.BlockSpec((ts, c), lambda b, i: (b * ns + i, 0)),
            pl.BlockSpec((ts, c), lambda b, i: (b * ns + i, 1)),
            pl.BlockSpec((CONV_HALO, c), halo_idx(0)),
            pl.BlockSpec((CONV_HALO, c), halo_idx(1)),
            pl.BlockSpec((ts, d), lambda b, i: (b * ns + i, 1)),
            pl.BlockSpec((k_taps, SUBLANES, c), lambda b, i: (0, 0, 0)),
            pl.BlockSpec((1, c), lambda b, i: (0, 0)),
            pl.BlockSpec((1, c), lambda b, i: (0, 0)),
            pl.BlockSpec((1, c), lambda b, i: (0, 0)),
            pl.BlockSpec((c, d), lambda b, i: (0, 0)),
            pl.BlockSpec((1, d), lambda b, i: (0, 0)),
        ],
        out_specs=pl.BlockSpec((ts, d), lambda b, i: (b * ns + i, 0)),
        out_shape=jax.ShapeDtypeStruct((t, d), F32),
        scratch_shapes=[
            pltpu.VMEM((ts + CONV_HALO + SUBLANES, c), F32),
            pltpu.VMEM((SUBLANES - 1, ts + CONV_HALO, c), F32),
            pltpu.VMEM((ts, c), F32),
        ],
        compiler_params=_params(2),
        name="conv_branch",
    )(proj, proj, proj, proj, proj, jnp.broadcast_to(w_dw[:, None, :], (k_taps, SUBLANES, c)),
      row(b_dw), row(ln_g), row(ln_b), w_pw, row(b_gate0))


def _ssm_kernel(u_ref, cg_ref, gate_ref, bmat_ref, lam_ref, cmat_ref, dskip_ref, wval_ref,
                wgate_ref, bg_ref, o_ref, st_ref, carry_ref):
    sl, w = u_ref.shape
    n_tiles = w // LANES
    hw = n_tiles // 2

    @pl.when(pl.program_id(1) == 0)
    def _():
        carry_ref[...] = jnp.zeros_like(carry_ref)

    for j in range(SUBLANES):
        bu = jnp.dot(u_ref[:, j * LANES:(j + 1) * LANES].astype(BF16), bmat_ref[j],
                     preferred_element_type=F32)
        for lt in range(n_tiles):
            st_ref[lt, pl.ds(j, sl, stride=SUBLANES), :] = bu[:, lt * LANES:(lt + 1) * LANES]

    lam = [lam_ref[:, lt * LANES:(lt + 1) * LANES] for lt in range(n_tiles)]

    def step(t, s):
        r = pl.multiple_of(t * SUBLANES, SUBLANES)
        new = [None] * n_tiles
        for q in range(hw):
            sr, si, ar, ai = s[q], s[q + hw], lam[q], lam[q + hw]
            new[q] = ar * sr - ai * si + st_ref[q, pl.ds(r, SUBLANES), :]
            new[q + hw] = ar * si + ai * sr + st_ref[q + hw, pl.ds(r, SUBLANES), :]
        for lt in range(n_tiles):
            st_ref[lt, pl.ds(r, SUBLANES), :] = new[lt]
        return tuple(new)

    s_end = lax.fori_loop(0, sl, step, tuple(carry_ref[lt] for lt in range(n_tiles)), unroll=4)
    for lt in range(n_tiles):
        carry_ref[lt] = s_end[lt]

    ys = []
    for j in range(SUBLANES):
        sj = jnp.concatenate([st_ref[lt, pl.ds(j, sl, stride=SUBLANES), :] for lt in range(n_tiles)],
                             axis=1)
        ys.append(jnp.dot(sj.astype(BF16), cmat_ref[j], preferred_element_type=F32))
    y = jnp.concatenate(ys, axis=1)

    y = y + dskip_ref[...] * u_ref[...]
    z = jax.nn.gelu(y).astype(BF16)
    val = jnp.dot(z, wval_ref[...], preferred_element_type=F32)
    gt = jnp.dot(z, wgate_ref[...], preferred_element_type=F32)
    branch = val * jax.nn.sigmoid(gt)
    o_ref[...] = cg_ref[...] + jax.nn.sigmoid(gate_ref[...] + bg_ref[...]) * branch


def _ssm_discretize_kernel(are_ref, aim_ref, ldt_ref, bre_ref, bim_ref, lre_ref, lim_ref, obr_ref, obi_ref):
    a_re = are_ref[...]
    a_im = aim_ref[...]
    dt = jnp.exp(ldt_ref[...])
    mag = jnp.exp(a_re * dt)
    l_re = mag * jnp.cos(a_im * dt)
    l_im = mag * jnp.sin(a_im * dt)
    den = a_re * a_re + a_im * a_im
    c_re = ((l_re - 1.0) * a_re + l_im * a_im) / den
    c_im = (l_im * a_re - (l_re - 1.0) * a_im) / den
    lre_ref[...] = l_re
    lim_ref[...] = l_im
    obr_ref[...] = c_re * bre_ref[...] - c_im * bim_ref[...]
    obi_ref[...] = c_re * bim_ref[...] + c_im * bre_ref[...]


def _ssm_tables(a_re, a_im, log_dt, b_re, b_im, c_re, c_im):
    g, p = a_re.shape
    gs = b_re.shape[-1]
    assert g % SUBLANES == 0
    gl = g // SUBLANES
    rows = lambda x: jnp.repeat(x.astype(F32), gs, axis=0)
    flat = lambda x: jnp.swapaxes(x.astype(F32), 1, 2).reshape(g * gs, p)
    shape = jax.ShapeDtypeStruct((g * gs, p), F32)
    l_re, l_im, bb_re, bb_im = pl.pallas_call(
        _ssm_discretize_kernel, out_shape=[shape] * 4, name="ssm_discretize",
    )(rows(a_re), rows(a_im), rows(jnp.broadcast_to(log_dt[:, None], (g, p))), flat(b_re), flat(b_im))

    pick = lambda x: x.reshape(g, gs, p)[:, 0, :].reshape(SUBLANES, gl * p)
    lam_t = jnp.concatenate([pick(l_re), pick(l_im)], axis=1)
    eye = jnp.eye(gl, dtype=F32)

    def bmap(x):
        x = x.reshape(SUBLANES, gl, gs, p)
        return jnp.einsum('jghp,gk->jghkp', x, eye).reshape(SUBLANES, gl * gs, gl * p)

    bmat = jnp.concatenate([bmap(bb_re), bmap(bb_im)], axis=2)

    def cmap(x):
        x = x.astype(F32).reshape(SUBLANES, gl, gs, p)
        return jnp.einsum('cghp,gk->cgpkh', x, eye).reshape(SUBLANES, gl * p, gl * gs)

    cmat = jnp.concatenate([cmap(c_re), -cmap(c_im)], axis=1)
    return lam_t, bmat.astype(BF16), cmat.astype(BF16)


def _ssm_branch(proj, conv_gated, bsz, seq, lam_t, bmat, cmat, d_skip, w_val, w_gate, b_gate1, sl):
    t = proj.shape[0]
    w = d_skip.shape[0]
    d = w_val.shape[1]
    ns = seq // sl
    n_tiles = w // LANES
    assert n_tiles == SUBLANES and d == 2 * w and seq % sl == 0
    u_col = proj.shape[1] // w - 1
    row = lambda a: a.reshape(1, -1)
    const = lambda b, i: (0, 0)
    const3 = lambda b, i: (0, 0, 0)
    return pl.pallas_call(
        _ssm_kernel,
        grid=(bsz, ns),
        in_specs=[
            pl.BlockSpec((sl, w), lambda b, i: (b * ns + i, u_col)),
            pl.BlockSpec((sl, d), lambda b, i: (b * ns + i, 0)),
            pl.BlockSpec((sl, d), lambda b, i: (b * ns + i, 2)),
            pl.BlockSpec((SUBLANES, LANES, w), const3),
            pl.BlockSpec((SUBLANES, w), const),
            pl.BlockSpec((SUBLANES, w, LANES), const3),
            pl.BlockSpec((1, w), const),
            pl.BlockSpec((w, d), const),
            pl.BlockSpec((w, d), const),
            pl.BlockSpec((1, d), const),
        ],
        out_specs=pl.BlockSpec((sl, d), lambda b, i: (b * ns + i, 0)),
        out_shape=jax.ShapeDtypeStruct((t, d), F32),
        scratch_shapes=[
            pltpu.VMEM((n_tiles, sl * SUBLANES, LANES), F32),
            pltpu.VMEM((n_tiles, SUBLANES, LANES), F32),
        ],
        compiler_params=_params(2),
        name="ssm_branch",
    )(proj, conv_gated, proj, bmat, lam_t, cmat, row(d_skip), w_val, w_gate, row(b_gate1))


def _out_proj_kernel(m_ref, x_ref, w_ref, g_ref, x2_ref, h2t_ref):
    x2 = x_ref[...] + jnp.dot(m_ref[...].astype(BF16), w_ref[...], preferred_element_type=F32)
    x2_ref[...] = x2
    h2t_ref[...] = _rms(x2, g_ref[...]).T.astype(BF16)


def _out_proj(merged, x2d, w, g, tm):
    t, d = x2d.shape
    return pl.pallas_call(
        _out_proj_kernel,
        grid=(t // tm,),
        in_specs=[
            pl.BlockSpec((tm, d), lambda i: (i, 0)),
            pl.BlockSpec((tm, d), lambda i: (i, 0)),
            pl.BlockSpec((d, d), lambda i: (0, 0)),
            pl.BlockSpec((1, d), lambda i: (0, 0)),
        ],
        out_specs=[pl.BlockSpec((tm, d), lambda i: (i, 0)), pl.BlockSpec((d, tm), lambda i: (0, i))],
        out_shape=[jax.ShapeDtypeStruct((t, d), F32), jax.ShapeDtypeStruct((d, t), BF16)],
        compiler_params=_params(1),
        name="out_proj",
    )(merged, x2d, w, g.reshape(1, d))


def _sort_pairs(n):
    pairs = []

    def merge(lo, cnt, r):
        step = 2 * r
        if step < cnt:
            merge(lo, cnt, step)
            merge(lo + r, cnt, step)
            pairs.extend((i, i + r) for i in range(lo + r, lo + cnt - r, step))
        else:
            pairs.append((lo, lo + r))

    def sort(lo, cnt):
        if cnt > 1:
            sort(lo, cnt // 2)
            sort(lo + cnt // 2, cnt // 2)
            merge(lo, cnt, 1)

    sort(0, n)
    return pairs


def _bitonic_pairs(n):
    pairs = []
    stride = n // 2
    while stride:
        pairs.extend((i, i + stride) for i in range(n) if not i & stride)
        stride //= 2
    return pairs


def _apply_pairs(v, pairs):
    v = list(v)
    for i, j in pairs:
        if v[j] is None:
            continue
        if v[i] is None:
            v[i], v[j] = v[j], None
        else:
            v[i], v[j] = jnp.maximum(v[i], v[j]), jnp.minimum(v[i], v[j])
    return v


def _top_sorted(v, k):
    v = _apply_pairs(v, _sort_pairs(k))
    shift = SUBLANES // 2
    while shift:
        w = [None if x is None else pltpu.roll(x, shift, axis=0) for x in v]
        merged = []
        for i in range(k):
            x, y = v[i], w[k - 1 - i]
            merged.append(y if x is None else x if y is None else jnp.maximum(x, y))
        v = _apply_pairs(merged, _bitonic_pairs(k))
        shift //= 2
    return v


def _stack_sublanes(rows, sub):
    out = rows[0]
    for j in range(1, len(rows)):
        out = jnp.where(sub == j, rows[j], out)
    return out


def _peer_query_kernel(h2t_ref, wq_ref, sk_ref, a1_ref, b2_ref, th_ref, qt_ref, s_ref):
    tt = h2t_ref.shape[1]
    n_hc, nk, half = sk_ref.shape
    n_heads = n_hc // 2
    k = PEER_TOPK
    assert k == 2 * SUBLANES and nk == k * SUBLANES

    qt_ref[...] = jnp.dot(wq_ref[...], h2t_ref[...], preferred_element_type=F32).astype(BF16)
    for hc in range(n_hc):
        s_ref[hc] = jnp.dot(sk_ref[hc], qt_ref[hc * half:(hc + 1) * half, :],
                            preferred_element_type=F32)

    sub = lax.broadcasted_iota(jnp.int32, (SUBLANES, LANES), 0)

    def sublane_reduce(x, op):
        shift = SUBLANES // 2
        while shift:
            x = op(x, pltpu.roll(x, shift, axis=0))
            shift //= 2
        return x

    def head(h, carry):
        for lt in range(tt // LANES):
            ls = slice(lt * LANES, (lt + 1) * LANES)
            tiles = lambda hc: [s_ref[hc, i * SUBLANES:(i + 1) * SUBLANES, ls] for i in range(k)]
            v1 = _top_sorted(tiles(2 * h), k)
            v2 = _top_sorted(tiles(2 * h + 1), k)
            lo2 = _stack_sublanes(v2[:SUBLANES], sub)
            hi2 = _stack_sublanes(v2[SUBLANES:], sub)
            hi1 = _stack_sublanes(v1[SUBLANES:], sub)
            parts = [(v1[0], lo2), (v1[0], hi2)] + [(v1[r], lo2) for r in range(1, SUBLANES)] \
                + [(hi1, v2[0])]
            cand = [p + q for p, q in parts]
            best = _top_sorted(cand + [None] * (k - len(cand)), k)
            m = best[0]
            tau = best[k - 1]
            picked = [c >= tau for c in cand]
            z = sum(jnp.where(pk, jnp.exp(c - m), 0.0) for pk, c in zip(picked, cand))
            z = sublane_reduce(z, jnp.add)
            shift = m * LOG2E + jnp.log2(z) + 1.0
            dd = [jnp.where(pk, (p * LOG2E - shift) + q * LOG2E, jnp.inf)
                  for pk, (p, q) in zip(picked, parts)]
            th = dd[0]
            for x in dd[1:]:
                th = jnp.minimum(th, x)
            th_ref[h, :, ls] = sublane_reduce(th, jnp.minimum)
            a1_ref[h, :, ls] = s_ref[2 * h, :, ls] * LOG2E - shift[0:1, :]
            b2_ref[h, :, ls] = s_ref[2 * h + 1, :, ls] * LOG2E
        return carry

    lax.fori_loop(0, n_heads, head, 0)


def _peer_query(h2t, wq_t, sk, tt):
    d, t = h2t.shape
    hq = wq_t.shape[0]
    n_hc, nk, half = sk.shape
    n_heads = n_hc // 2
    return pl.pallas_call(
        _peer_query_kernel,
        grid=(t // tt,),
        in_specs=[
            pl.BlockSpec((d, tt), lambda i: (0, i)),
            pl.BlockSpec((hq, d), lambda i: (0, 0)),
            pl.BlockSpec((n_hc, nk, half), lambda i: (0, 0, 0)),
        ],
        out_specs=[
            pl.BlockSpec((n_heads, nk, tt), lambda i: (0, 0, i)),
            pl.BlockSpec((n_heads, nk, tt), lambda i: (0, 0, i)),
            pl.BlockSpec((n_heads, SUBLANES, tt), lambda i: (0, 0, i)),
        ],
        out_shape=[
            jax.ShapeDtypeStruct((n_heads, nk, t), F32),
            jax.ShapeDtypeStruct((n_heads, nk, t), F32),
            jax.ShapeDtypeStruct((n_heads, SUBLANES, t), F32),
        ],
        scratch_shapes=[
            pltpu.VMEM((hq, tt), BF16),
            pltpu.VMEM((n_hc, nk, tt), F32),
        ],
        compiler_params=_params(1),
        name="peer_query",
    )(h2t, wq_t, sk)


GELU_C1 = math.sqrt(2.0 / math.pi)
GELU_C3 = GELU_C1 * 0.044715


def _zero_after(x):
    bits = pltpu.bitcast(x, jnp.uint32)
    bits = lax.shift_right_logical(lax.shift_right_logical(bits, jnp.uint32(16)), jnp.uint32(16))
    return pltpu.bitcast(bits, F32)


def _expert_weights(act_ref, w_ref, a1_ref, b2_ref, th_ref, i1_row0, anchors):
    e_tile, tt = act_ref.shape
    n_heads, nk, _ = b2_ref.shape
    pair = 2 * SUBLANES
    blocks = [(lt, i1) for lt in range(tt // LANES) for i1 in range(e_tile // nk)]
    for (lt, i1), anchor in zip(blocks, anchors):
        ls = slice(lt * LANES, (lt + 1) * LANES)
        zero = _zero_after(anchor)
        th = [th_ref[h, :, ls] + zero for h in range(n_heads)]
        a = [jnp.broadcast_to(a1_ref[h, i1_row0 + i1:i1_row0 + i1 + 1, ls], (SUBLANES, LANES))
             for h in range(n_heads)]
        for blk in range(nk // pair):
            halves = []
            for half in range(2):
                r2 = blk * pair + half * SUBLANES
                g = None
                for h in range(n_heads):
                    dd = a[h] + b2_ref[h, r2:r2 + SUBLANES, ls]
                    term = jnp.where(dd >= th[h], jnp.exp2(dd), 0.0)
                    g = term if g is None else g + term
                x = act_ref[i1 * nk + r2:i1 * nk + r2 + SUBLANES, ls]
                halves.append((g * x) * (1.0 + jnp.tanh(x * (GELU_C1 + GELU_C3 * (x * x)))))
            r = i1 * nk + blk * pair
            w_ref[r:r + pair, ls] = jnp.concatenate(halves, axis=0).astype(BF16)


def _peer_expert_kernel(h2t_ref, a1p_ref, b2p_ref, thp_ref, a1c_ref, b2c_ref, thc_ref, u_ref, vt_ref,
                        x2_ref, gf_ref, o_ref, acc_ref, act0_ref, act1_ref, w0_ref, w1_ref, *, n_e):
    g = pl.program_id(0)
    e_tile, tt = act0_ref.shape
    nk = b2c_ref.shape[1]
    n_i1 = e_tile // nk

    @pl.when(g == 0)
    def _():
        acc_ref[...] = jnp.zeros_like(acc_ref)
        act1_ref[...] = jnp.zeros_like(act1_ref)
        w0_ref[...] = jnp.zeros_like(w0_ref)

    def tick(u_rows, vt_cols, act_new, act_old, w_old, w_new, a1_ref, b2_ref, th_ref, i1_row0):
        half = (tt // LANES) * n_i1 // 2
        new_act = jnp.dot(u_ref[u_rows, :], h2t_ref[...], preferred_element_type=F32)
        upd = jnp.dot(vt_ref[:, vt_cols], w_old[...], preferred_element_type=F32)
        anchors = [new_act[0:SUBLANES, 0:LANES]] * half
        anchors += [new_act[0:SUBLANES, tt // 2:tt // 2 + LANES]] * half
        _expert_weights(act_old, w_new, a1_ref, b2_ref, th_ref, i1_row0, anchors)
        act_new[...] = new_act
        acc_ref[...] += upd

    tick(slice(0, e_tile), slice(0, e_tile), act0_ref, act1_ref, w0_ref, w1_ref,
         a1p_ref, b2p_ref, thp_ref, n_i1)
    tick(slice(e_tile, 2 * e_tile), slice(e_tile, 2 * e_tile), act1_ref, act0_ref, w1_ref, w0_ref,
         a1c_ref, b2c_ref, thc_ref, 0)

    @pl.when(jnp.logical_and(g > 0, (2 * g) % n_e == 0))
    def _():
        x3 = x2_ref[...] + acc_ref[...].T
        o_ref[...] = _rms(x3, gf_ref[...])
        acc_ref[...] = jnp.zeros_like(acc_ref)


def _peer_expert(h2t, a1, b2, th, u, vt, x2, g_final, tt, e_tile):
    d, t = h2t.shape
    n_heads, nk, _ = a1.shape
    n_exp = u.shape[0]
    n_t = t // tt
    n_e = n_exp // e_tile
    assert n_e % 2 == 0 and 2 * e_tile == SUBLANES * nk
    half = n_e // 2
    steps = n_t * half + 1

    tok_a = lambda g: jnp.minimum((2 * g) // n_e, n_t - 1)
    tok_p = lambda g: jnp.maximum(2 * g - 1, 0) // n_e
    tok_c = lambda g: jnp.maximum(2 * g - 2, 0) // n_e
    meta = lambda shape, tok: pl.BlockSpec(shape, lambda g: (0, 0, tok(g)))
    a1_spec = lambda tick, tok: pl.BlockSpec((n_heads, SUBLANES, tt),
                                             lambda g: (0, (tick(g) % n_e) // 2, tok(g)))
    return pl.pallas_call(
        functools.partial(_peer_expert_kernel, n_e=n_e),
        grid=(steps,),
        in_specs=[
            pl.BlockSpec((d, tt), lambda g: (0, tok_a(g))),
            a1_spec(lambda g: 2 * g - 1, tok_p), meta((n_heads, nk, tt), tok_p), meta((n_heads, SUBLANES, tt), tok_p),
            a1_spec(lambda g: 2 * g, tok_a), meta((n_heads, nk, tt), tok_a), meta((n_heads, SUBLANES, tt), tok_a),
            pl.BlockSpec((2 * e_tile, d), lambda g: (g % half, 0)),
            pl.BlockSpec((d, 2 * e_tile), lambda g: (0, jnp.maximum(g - 1, 0) % half)),
            pl.BlockSpec((tt, d), lambda g: (tok_c(g), 0)),
            pl.BlockSpec((1, d), lambda g: (0, 0)),
        ],
        out_specs=pl.BlockSpec((tt, d), lambda g: (tok_c(g), 0)),
        out_shape=jax.ShapeDtypeStruct((t, d), F32),
        scratch_shapes=[
            pltpu.VMEM((d, tt), F32),
            pltpu.VMEM((e_tile, tt), F32),
            pltpu.VMEM((e_tile, tt), F32),
            pltpu.VMEM((e_tile, tt), BF16),
            pltpu.VMEM((e_tile, tt), BF16),
        ],
        compiler_params=_params(1),
        name="peer_expert",
    )(h2t, a1, b2, th, a1, b2, th, u, vt, x2, g_final.reshape(1, d))


def _cast_transpose_kernel(v_ref, o_ref):
    o_ref[...] = v_ref[...].T.astype(BF16)


def _cast_transpose(v, rows):
    n, d = v.shape
    return pl.pallas_call(
        _cast_transpose_kernel,
        grid=(n // rows,),
        in_specs=[pl.BlockSpec((rows, d), lambda i: (i, 0))],
        out_specs=pl.BlockSpec((d, rows), lambda i: (0, i)),
        out_shape=jax.ShapeDtypeStruct((d, n), BF16),
        compiler_params=_params(1),
        name="cast_transpose",
    )(v)


IN_PROJ_ROWS, IN_PROJ_COLS = 1024, 1792
CONV_SEQ = 256
SSM_SEQ = 256
OUT_PROJ_ROWS = 512
QUERY_TOKENS = 512
EXPERT_TOKENS = 512
TRANSPOSE_ROWS = 1024
EXPERT_I1_ROWS = SUBLANES // 2


def _tile(n, pref):
    t = min(n, pref)
    assert n % t == 0
    return t


def kernel(x, norm_mix, w_in, b_gate, conv_w_dw, conv_b_dw, conv_ln_g, conv_ln_b, conv_w_out, ssm_a_re, ssm_a_im, ssm_log_dt, ssm_b_re, ssm_b_im, ssm_c_re, ssm_c_im, ssm_d, ssm_w_val, ssm_w_gate, w_out, norm_ffn, peer_w_q, peer_sub_keys, peer_u, peer_v, norm_final):
    bsz, seq, d = x.shape
    depth = w_in.shape[0]
    t = bsz * seq
    cw = conv_w_dw.shape[-1]
    sw = ssm_d.shape[-1]
    n_heads, _, nk, half = peer_sub_keys.shape[1:]
    assert 2 * cw == d and 2 * sw == d and peer_u.shape[1] == nk * nk

    assert depth == 1, "the final rmsnorm is fused into the PEER expert stage"
    l = 0
    x2d = x.reshape(t, d).astype(F32)
    w = w_in[l]
    w_perm = jnp.concatenate([w[:, :2 * cw], w[:, 2 * cw + sw:], w[:, 2 * cw:2 * cw + sw]],
                             axis=1).astype(BF16)
    proj = _in_proj(x2d, norm_mix[l], w_perm, _tile(t, IN_PROJ_ROWS), IN_PROJ_COLS)

    conv_gated = _conv_branch(proj, bsz, seq, conv_w_dw[l], conv_b_dw[l], conv_ln_g[l],
                              conv_ln_b[l], conv_w_out[l].astype(BF16), b_gate[l, :d],
                              _tile(seq, CONV_SEQ))

    lam_t, bmat, cmat = _ssm_tables(ssm_a_re[l], ssm_a_im[l], ssm_log_dt[l], ssm_b_re[l],
                                    ssm_b_im[l], ssm_c_re[l], ssm_c_im[l])
    merged = _ssm_branch(proj, conv_gated, bsz, seq, lam_t, bmat, cmat, ssm_d[l],
                         ssm_w_val[l].astype(BF16), ssm_w_gate[l].astype(BF16), b_gate[l, d:],
                         _tile(seq, SSM_SEQ))

    x2, h2t = _out_proj(merged, x2d, w_out[l].astype(BF16), norm_ffn[l], _tile(t, OUT_PROJ_ROWS))

    a1, b2, th = _peer_query(h2t, peer_w_q[l].T.astype(BF16),
                             peer_sub_keys[l].reshape(2 * n_heads, nk, half).astype(BF16),
                             _tile(t, QUERY_TOKENS))
    out = _peer_expert(h2t, a1, b2, th, peer_u[l].astype(BF16), _cast_transpose(peer_v[l], _tile(nk * nk, TRANSPOSE_ROWS)), x2,
                       norm_final, _tile(t, EXPERT_TOKENS), EXPERT_I1_ROWS * nk)
    return out.reshape(bsz, seq, d).astype(x.dtype)
```
